```python
import math
import jax
import jax.numpy as jnp
from jax import lax
import numpy as np

D_MODEL = 1024
BATCH = 16
SEQ = 2048
DEPTH = 2

CTX_LEN = 256
GRID_W = 64

HG_HEADS = 4
HG_DK = 128
HG_DV = 128
HG_CHUNK = 64
LB_FLOOR = 1e-30

DA_HEADS = 4
DA_DH = 64
DA_DV = 2 * DA_DH
Q_BLOCK = 128
ROPE_THETA = 10000.0

FR_GROUPS = 4
FR_DG = 128

N_EXPERTS = 64
TOP_K = 8
EXPERT_DIM = 256
SHARED_DIM = 256
N_GROUPS = 8
TOPK_GROUPS = 4
ROUTED_SCALE = 2.5
ROW_BLOCK = 256
MASK_VALUE = -1e9

N_BRANCHES = 3
HG_W = HG_HEADS * HG_DK
HG_VW = HG_HEADS * HG_DV
DA_QKW = DA_HEADS * 2 * DA_DH
DA_VW = DA_HEADS * DA_DV
FR_W = FR_GROUPS * FR_DG
GATE_W = N_BRANCHES * D_MODEL
IN_SPLITS = (HG_W, HG_W, HG_W, HG_VW, HG_VW, DA_QKW, DA_QKW, DA_VW, FR_W, GATE_W)
IN_WIDTH = sum(IN_SPLITS)

DEEPNORM_ALPHA = (2 * DEPTH) ** 0.25
DEEPNORM_BETA = (8 * DEPTH) ** -0.25
ADA_INIT = 0.5
LN_EPS = 1e-5

kernel_name = 'hybrid_hgrn2_diffattn_fnet_moe_dit'


def layer_norm(x, g=None, b=None):
    xf = x.astype(jnp.float32)
    mu = jnp.mean(xf, axis=-1, keepdims=True)
    var = jnp.mean(jnp.square(xf - mu), axis=-1, keepdims=True)
    y = (xf - mu) * lax.rsqrt(var + LN_EPS)
    if g is not None:
        y = y * g.astype(jnp.float32) + b.astype(jnp.float32)
    return y.astype(x.dtype)


def rms_norm(x, g):
    xf = x.astype(jnp.float32)
    y = xf * lax.rsqrt(jnp.mean(jnp.square(xf), axis=-1, keepdims=True) + LN_EPS) * g.astype(jnp.float32)
    return y.astype(x.dtype)


def modulate(x, shift, scale):
    return layer_norm(x) * (1 + scale) + shift


def axial_rope(seq_len, dtype):
    rows = seq_len // GRID_W
    row = jnp.repeat(jnp.arange(rows, dtype=jnp.float32), GRID_W)
    col = (jnp.arange(seq_len) % GRID_W).astype(jnp.float32)
    n_freq = DA_DH // 4
    inv = ROPE_THETA ** (-jnp.arange(n_freq, dtype=jnp.float32) / n_freq)
    ang = jnp.concatenate([row[:, None] * inv, col[:, None] * inv], axis=-1)
    ang = jnp.concatenate([ang, ang], axis=-1)
    return jnp.cos(ang).astype(dtype), jnp.sin(ang).astype(dtype)


def apply_rope(x, cos, sin):
    half = DA_DH // 2
    rot = jnp.concatenate([-x[..., half:], x[..., :half]], axis=-1)
    return x * cos[:, None, None, :] + rot * sin[:, None, None, :]


def hgrn_log_forget(f_raw, lb):
    lbf = lb.astype(jnp.float32)
    return jnp.logaddexp(jnp.log(jnp.maximum(lbf, LB_FLOOR)),
                         jnp.log1p(-lbf) + jax.nn.log_sigmoid(f_raw.astype(jnp.float32)))


def to_chunks(a):
    B, L, H, d = a.shape
    return a.reshape(B, L // HG_CHUNK, HG_CHUNK, H, d).transpose(1, 0, 3, 2, 4)


def from_chunks(a):
    n, B, H, C, d = a.shape
    return a.transpose(1, 0, 3, 2, 4).reshape(B, n * C, H, d)


def hgrn_chunk_scan(q, k, v, log_f, s0):
    causal = jnp.tril(jnp.ones((HG_CHUNK, HG_CHUNK), dtype=bool))

    def step(s, inp):
        qc, kc, vc, gc = inp
        b = jnp.cumsum(gc, axis=2)
        diff = b[:, :, :, None, :] - b[:, :, None, :, :]
        decay = jnp.where(causal[:, :, None], jnp.exp(jnp.minimum(diff, 0.0)), 0.0)
        scores = jnp.einsum('bhtd,bhsd,bhtsd->bhts', qc, kc, decay)
        o = jnp.einsum('bhts,bhse->bhte', scores, vc) + jnp.einsum('bhtd,bhde->bhte', qc * jnp.exp(b), s)
        b_end = b[:, :, -1:, :]
        s_new = jnp.exp(b_end[:, :, 0, :, None]) * s + jnp.einsum('bhsd,bhse->bhde', kc * jnp.exp(b_end - b), vc)
        return s_new, o

    xs = tuple(to_chunks(a.astype(jnp.float32)) for a in (q, k, v, log_f))
    s_fin, o = lax.scan(step, s0, xs)
    return from_chunks(o), s_fin


def hgrn_direction(q_c, v_c, f_c, q_l, v_l, f_l, lb, reverse):
    flip = (lambda a: jnp.flip(a, axis=1)) if reverse else (lambda a: a)

    def run(q, v, f, s0):
        B, L = q.shape[:2]
        log_f = hgrn_log_forget(f, lb).reshape(B, L, HG_HEADS, HG_DK)
        k = -jnp.expm1(log_f)
        o, s = hgrn_chunk_scan(flip(q), flip(k), flip(v), flip(log_f), s0)
        return flip(o), s

    s0 = jnp.zeros((q_c.shape[0], HG_HEADS, HG_DK, HG_DV), jnp.float32)
    o_c, s_c = run(q_c, v_c, f_c, s0)
    o_l, _ = run(q_l, v_l, f_l, s_c)
    return o_c, o_l


def hgrn_readout(o_f, o_b, gate, norm_g):
    B, L = gate.shape[:2]
    o = rms_norm(o_f + o_b, norm_g)
    g = jax.nn.silu(gate.astype(jnp.float32)).reshape(B, L, HG_HEADS, HG_DV)
    return (o * g).reshape(B, L, HG_VW).astype(gate.dtype)


def diff_heads(qr, kr, vr):
    B, L = qr.shape[:2]
    return (qr.reshape(B, L, DA_HEADS, 2, DA_DH), kr.reshape(B, L, DA_HEADS, 2, DA_DH),
            vr.reshape(B, L, DA_HEADS, DA_DV))


def diff_softmax_attend(q, k, v, lam):
    s = jnp.einsum('bqhcd,bkhcd->bhcqk', q, k, preferred_element_type=jnp.float32) * (DA_DH ** -0.5)
    p = jax.nn.softmax(s, axis=-1)
    pd = p[:, :, 0] - lam * p[:, :, 1]
    return jnp.einsum('bhqk,bkhe->bqhe', pd.astype(v.dtype), v)


def diff_attention_latent(q, k_all, v_all, lam):
    B, L = q.shape[:2]
    nb = L // Q_BLOCK
    qb = q.reshape(B, nb, Q_BLOCK, DA_HEADS, 2, DA_DH).swapaxes(0, 1)
    o = lax.map(lambda qq: diff_softmax_attend(qq, k_all, v_all, lam), qb)
    return o.swapaxes(0, 1).reshape(B, L, DA_HEADS, DA_DV)


def diff_readout(o, norm_g, lam_init):
    B, L = o.shape[:2]
    return (rms_norm(o, norm_g) * (1 - lam_init)).reshape(B, L, DA_VW)


def fourier_mix(u):
    B, L, _ = u.shape
    ug = u.reshape(B, L, FR_GROUPS, FR_DG).astype(jnp.float32)
    y = jnp.fft.fft2(ug, axes=(1, 3), norm='ortho').real
    return y.reshape(B, L, FR_W).astype(u.dtype)


def gated_merge(a, b, cf, gates, w_pa, w_pb, w_pc, w_o):
    g = jax.nn.sigmoid(gates.astype(jnp.float32)).astype(a.dtype)
    ga, gb, gc = jnp.split(g, N_BRANCHES, axis=-1)
    m = ga * (a @ w_pa) + gb * (b @ w_pb) + gc * (cf @ w_pc)
    return m @ w_o


def mixer_block(parts_c, parts_l, cos, sin, lb, hg_g, lam, lam_init, da_g, w_pa, w_pb, w_pc, w_o, need_ctx):
    qh_c, ff_c, fb_c, i_c, og_c, qa_c, ka_c, va_c, u_c, gt_c = parts_c
    qh_l, ff_l, fb_l, i_l, og_l, qa_l, ka_l, va_l, u_l, gt_l = parts_l

    def hg_qv(qh, i):
        B, L = qh.shape[:2]
        return jax.nn.silu(qh).reshape(B, L, HG_HEADS, HG_DK), i.reshape(B, L, HG_HEADS, HG_DV)

    q_c, v_c = hg_qv(qh_c, i_c)
    q_l, v_l = hg_qv(qh_l, i_l)
    of_c, of_l = hgrn_direction(q_c, v_c, ff_c, q_l, v_l, ff_l, lb[0], False)
    ob_c, ob_l = hgrn_direction(q_c, v_c, fb_c, q_l, v_l, fb_l, lb[1], True)
    a_l = hgrn_readout(of_l, ob_l, og_l, hg_g)

    qd_c, kd_c, vd_c = diff_heads(qa_c, ka_c, va_c)
    qd_l, kd_l, vd_l = diff_heads(qa_l, ka_l, va_l)
    qd_l = apply_rope(qd_l, cos, sin)
    kd_l = apply_rope(kd_l, cos, sin)
    k_all = jnp.concatenate([kd_c, kd_l], axis=1)
    v_all = jnp.concatenate([vd_c, vd_l], axis=1)
    b_l = diff_readout(diff_attention_latent(qd_l, k_all, v_all, lam), da_g, lam_init)

    c_l = fourier_mix(u_l)
    mix_l = gated_merge(a_l, b_l, c_l, gt_l, w_pa, w_pb, w_pc, w_o)
    if not need_ctx:
        return None, mix_l

    a_c = hgrn_readout(of_c, ob_c, og_c, hg_g)
    b_c = diff_readout(diff_softmax_attend(qd_c, kd_c, vd_c, lam), da_g, lam_init)
    c_c = fourier_mix(u_c)
    mix_c = gated_merge(a_c, b_c, c_c, gt_c, w_pa, w_pb, w_pc, w_o)
    return mix_c, mix_l


def swiglu(t, w_g, w_u, w_d):
    return (jax.nn.silu(t @ w_g) * (t @ w_u)) @ w_d


def grouped_experts(t, eidx, w, w_g, w_u, w_d):
    N, D = t.shape
    nk = eidx.size
    flat_e = eidx.reshape(-1)
    order = jnp.argsort(flat_e)
    e_sorted = flat_e[order]
    tok_sorted = (order // TOP_K).astype(jnp.int32)
    w_sorted = w.reshape(-1)[order]
    counts = jnp.bincount(flat_e, length=N_EXPERTS)
    padded = (counts + ROW_BLOCK - 1) // ROW_BLOCK * ROW_BLOCK
    pad_end = jnp.cumsum(padded)
    pad_start = pad_end - padded
    grp_start = jnp.cumsum(counts) - counts
    dest = pad_start[e_sorted] + jnp.arange(nk, dtype=jnp.int32) - grp_start[e_sorted]
    n_blocks = -(-nk // ROW_BLOCK) + N_EXPERTS
    n_rows = n_blocks * ROW_BLOCK
    row_tok = jnp.full((n_rows,), N, jnp.int32).at[dest].set(tok_sorted)
    row_w = jnp.zeros((n_rows,), t.dtype).at[dest].set(w_sorted)
    block_e = jnp.minimum(jnp.searchsorted(pad_end, jnp.arange(n_blocks, dtype=jnp.int32) * ROW_BLOCK, side='right'), N_EXPERTS - 1)
    t_ext = jnp.concatenate([t, jnp.zeros((1, D), t.dtype)], axis=0)

    def step(acc, blk):
        tok, wt, e = blk
        xb = t_ext[tok]
        hb = jax.nn.silu(xb @ w_g[e]) * (xb @ w_u[e])
        return acc.at[tok].add((hb @ w_d[e]) * wt[:, None]), None

    acc, _ = lax.scan(step, jnp.zeros((N + 1, D), t.dtype),
                      (row_tok.reshape(n_blocks, ROW_BLOCK), row_w.reshape(n_blocks, ROW_BLOCK), block_e))
    return acc[:N]


def moe_ffn(t, w_router, router_bias, w_eg, w_eu, w_ed, w_sg, w_su, w_sd):
    N = t.shape[0]
    per = N_EXPERTS // N_GROUPS
    scores = jax.nn.sigmoid(jnp.matmul(t.astype(jnp.float32), w_router.astype(jnp.float32)))
    sel = scores + router_bias.astype(jnp.float32)
    gscore = lax.top_k(sel.reshape(N, N_GROUPS, per), 2)[0].sum(-1)
    _, gidx = lax.top_k(gscore, TOPK_GROUPS)
    gmask = jnp.any(gidx[:, :, None] == jnp.arange(N_GROUPS)[None, None, :], axis=1)
    sel = jnp.where(jnp.repeat(gmask, per, axis=1), sel, MASK_VALUE)
    _, eidx = lax.top_k(sel, TOP_K)
    w = jnp.take_along_axis(scores, eidx, axis=1)
    w = (w / jnp.sum(w, axis=-1, keepdims=True) * ROUTED_SCALE).astype(t.dtype)
    return grouped_experts(t, eidx, w, w_eg, w_eu, w_ed) + swiglu(t, w_sg, w_su, w_sd)


def setup_inputs(seed: int = 0) -> dict:
    key = jax.random.key(seed)
    ks = jax.random.split(key, 27)
    D = D_MODEL
    beta = DEEPNORM_BETA

    def nrm(k, shape, s):
        return jax.random.normal(k, shape, jnp.float32) * s

    return {
        'x': nrm(ks[0], (BATCH, SEQ, D), 1.0),
        'c': nrm(ks[1], (BATCH, D), 1.0),
        'ctx': nrm(ks[2], (BATCH, CTX_LEN, D), 1.0),
        'c_ctx': nrm(ks[3], (D,), 1.0),
        'w_ada': nrm(ks[4], (DEPTH, D, 6 * D), ADA_INIT * D ** -0.5),
        'b_ada': nrm(ks[5], (DEPTH, 6 * D), 0.02),
        'w_in': nrm(ks[6], (DEPTH, D, IN_WIDTH), D ** -0.5),
        'lb_logits': nrm(ks[7], (DEPTH, 2, HG_W), 0.5),
        'hg_norm_g': 1.0 + nrm(ks[8], (DEPTH, HG_DV), 0.02),
        'lam_params': nrm(ks[9], (DEPTH, 4, DA_DH), 0.1),
        'da_norm_g': 1.0 + nrm(ks[10], (DEPTH, DA_DV), 0.02),
        'w_branch_a': nrm(ks[11], (DEPTH, HG_VW, D), HG_VW ** -0.5),
        'w_branch_b': nrm(ks[12], (DEPTH, DA_VW, D), DA_VW ** -0.5),
        'w_branch_c': nrm(ks[13], (DEPTH, FR_W, D), FR_W ** -0.5),
        'w_out': nrm(ks[14], (DEPTH, D, D), beta * D ** -0.5),
        'ln1_g': 1.0 + nrm(ks[15], (DEPTH, D), 0.02),
        'ln1_b': nrm(ks[16], (DEPTH, D), 0.02),
        'ln2_g': 1.0 + nrm(ks[17], (DEPTH, D), 0.02),
        'ln2_b': nrm(ks[18], (DEPTH, D), 0.02),
        'w_router': nrm(ks[19], (DEPTH, D, N_EXPERTS), D ** -0.5),
        'router_bias': nrm(ks[20], (DEPTH, N_EXPERTS), 0.01),
        'w_exp_gate': nrm(ks[21], (DEPTH, N_EXPERTS, D, EXPERT_DIM), D ** -0.5),
        'w_exp_up': nrm(ks[22], (DEPTH, N_EXPERTS, D, EXPERT_DIM), D ** -0.5),
        'w_exp_down': nrm(ks[23], (DEPTH, N_EXPERTS, EXPERT_DIM, D), beta * EXPERT_DIM ** -0.5),
        'w_sh_gate': nrm(ks[24], (DEPTH, D, SHARED_DIM), D ** -0.5),
        'w_sh_up': nrm(ks[25], (DEPTH, D, SHARED_DIM), D ** -0.5),
        'w_sh_down': nrm(ks[26], (DEPTH, SHARED_DIM, D), beta * SHARED_DIM ** -0.5),
    }


def reference(x, c, ctx, c_ctx, w_ada, b_ada, w_in, lb_logits, hg_norm_g, lam_params, da_norm_g,
              w_branch_a, w_branch_b, w_branch_c, w_out, ln1_g, ln1_b, ln2_g, ln2_b,
              w_router, router_bias, w_exp_gate, w_exp_up, w_exp_down, w_sh_gate, w_sh_up, w_sh_down):
    B, L, D = x.shape
    Lc = ctx.shape[1]
    cos, sin = axial_rope(L, x.dtype)
    split_pts = [sum(IN_SPLITS[:i + 1]) for i in range(len(IN_SPLITS) - 1)]
    sm = jax.nn.softmax(lb_logits.astype(jnp.float32), axis=0)
    lower = jnp.cumsum(sm, axis=0) - sm[0:1]
    silu_c = jax.nn.silu(c)
    silu_cc = jax.nn.silu(c_ctx)
    xc = ctx
    for l in range(DEPTH):
        need_ctx = l < DEPTH - 1
        mod_l = jnp.split((silu_c @ w_ada[l] + b_ada[l])[:, None, :], 6, axis=-1)
        mod_c = jnp.split(silu_cc @ w_ada[l] + b_ada[l], 6, axis=-1)
        h_l = modulate(x, mod_l[0], mod_l[1])
        h_c = modulate(xc, mod_c[0], mod_c[1])
        parts_l = jnp.split(h_l @ w_in[l], split_pts, axis=-1)
        parts_c = jnp.split(h_c @ w_in[l], split_pts, axis=-1)
        lam_init = 0.8 - 0.6 * math.exp(-0.3 * l)
        lp = lam_params[l].astype(jnp.float32)
        lam = jnp.exp(jnp.sum(lp[0] * lp[1])) - jnp.exp(jnp.sum(lp[2] * lp[3])) + lam_init
        mix_c, mix_l = mixer_block(parts_c, parts_l, cos, sin, lower[l], hg_norm_g[l], lam, lam_init,
                                   da_norm_g[l], w_branch_a[l], w_branch_b[l], w_branch_c[l], w_out[l],
                                   need_ctx)
        x = layer_norm(DEEPNORM_ALPHA * x + mod_l[2] * mix_l, ln1_g[l], ln1_b[l])
        h2_l = modulate(x, mod_l[3], mod_l[4])
        if need_ctx:
            xc = layer_norm(DEEPNORM_ALPHA * xc + mod_c[2] * mix_c, ln1_g[l], ln1_b[l])
            h2_c = modulate(xc, mod_c[3], mod_c[4])
            tokens = jnp.concatenate([h2_c.reshape(-1, D), h2_l.reshape(-1, D)], axis=0)
        else:
            tokens = h2_l.reshape(-1, D)
        y = moe_ffn(tokens, w_router[l], router_bias[l], w_exp_gate[l], w_exp_up[l], w_exp_down[l],
                    w_sh_gate[l], w_sh_up[l], w_sh_down[l])
        x = layer_norm(DEEPNORM_ALPHA * x + mod_l[5] * y[-B * L:].reshape(B, L, D), ln2_g[l], ln2_b[l])
        if need_ctx:
            xc = layer_norm(DEEPNORM_ALPHA * xc + mod_c[5] * y[:B * Lc].reshape(B, Lc, D), ln2_g[l], ln2_b[l])
    return x
```

```python
import functools
import math

import numpy as np
import jax
import jax.numpy as jnp
from jax import lax
from jax.experimental import pallas as pl
from jax.experimental.pallas import tpu as pltpu

F32 = jnp.float32
BF16 = jnp.bfloat16

HG_HEADS = 4
HG_DK = 128
LB_FLOOR = 1e-30
DA_HEADS = 4
DA_DH = 64
ROPE_THETA = 10000.0
GRID_W = 64
FR_GROUPS = 4
FR_DG = 128
N_EXPERTS = 64
TOP_K = 8
EXPERT_DIM = 256
N_GROUPS = 8
TOPK_GROUPS = 4
ROUTED_SCALE = 2.5
MASK_VALUE = -1e9
LN_EPS = 1e-5

HEAD_W = 128
BRANCH_W = 512
GATES_BLOCKS = 6
COL_Q, COL_FF, COL_FB, COL_I, COL_OG, COL_QA, COL_KA, COL_VA, COL_U = range(6, 15)

ROW_TILE = 256
HG_CHUNK = 128
HG_LEVELS = (64, 32, 16, 8, 4, 2, 1)
IN_N_TILE = 2560
MOE_TILE = 1024
ROUTER_TILE = 512
VMEM_LIMIT = 48 * 1024 * 1024


def _cparams(sem):
    return pltpu.CompilerParams(dimension_semantics=sem, vmem_limit_bytes=VMEM_LIMIT)


def _dot(a, b):
    return jnp.dot(a, b, preferred_element_type=F32)


def _dot_nt(a, b):
    return lax.dot_general(a, b, (((1,), (1,)), ((), ())), preferred_element_type=F32)


def _dot_tn(a, b):
    return lax.dot_general(a, b, (((0,), (0,)), ((), ())), preferred_element_type=F32)


def _silu(x):
    return x * jax.nn.sigmoid(x)


def _ln(x):
    mu = jnp.mean(x, axis=-1, keepdims=True)
    xc = x - mu
    var = jnp.mean(xc * xc, axis=-1, keepdims=True)
    return xc * lax.rsqrt(var + LN_EPS)


def _ada_kernel(c_ref, w_ref, b_ref, o_ref):
    o_ref[...] = jnp.dot(_silu(c_ref[...]), w_ref[...], preferred_element_type=F32,
                         precision=lax.Precision.HIGHEST) + b_ref[...]


def _ada(cvec, w, b):
    rows, d = cvec.shape
    n = w.shape[1]
    tn = 1536
    return pl.pallas_call(
        _ada_kernel,
        grid=(n // tn,),
        in_specs=[pl.BlockSpec((rows, d), lambda j: (0, 0)),
                  pl.BlockSpec((d, tn), lambda j: (0, j)),
                  pl.BlockSpec((1, tn), lambda j: (0, j))],
        out_specs=pl.BlockSpec((rows, tn), lambda j: (0, j)),
        out_shape=jax.ShapeDtypeStruct((rows, n), F32),
        compiler_params=_cparams(("parallel",)),
        name="ada",
    )(cvec, w, b.reshape(1, n))


def _inproj_kernel(x_ref, mod_ref, w_ref, o_ref):
    shift = mod_ref[0, 0, 0:1, :]
    scale = mod_ref[0, 0, 1:2, :]
    h = _ln(x_ref[0]) * (1.0 + scale) + shift
    o_ref[0] = _dot(h.astype(BF16), w_ref[...])


def _inproj(x, mods, w, n_ctx_tiles):
    bsz, t, d = x.shape
    n = w.shape[1]
    return pl.pallas_call(
        _inproj_kernel,
        grid=(n // IN_N_TILE, bsz, t // ROW_TILE),
        in_specs=[pl.BlockSpec((1, ROW_TILE, d), lambda j, b, i: (b, i, 0)),
                  pl.BlockSpec((1, 1, 6, d), lambda j, b, i: (b, (i >= n_ctx_tiles).astype(jnp.int32), 0, 0)),
                  pl.BlockSpec((d, IN_N_TILE), lambda j, b, i: (0, j))],
        out_specs=pl.BlockSpec((1, ROW_TILE, IN_N_TILE), lambda j, b, i: (b, i, j)),
        out_shape=jax.ShapeDtypeStruct((bsz, t, n), F32),
        compiler_params=_cparams(("parallel", "parallel", "parallel")),
        name="inproj",
    )(x, mods, w)


def _hgrn_tables(reverse):
    c = HG_CHUNK
    t = np.arange(c)
    if reverse:
        cum = (t[None, :] >= t[:, None])
    else:
        cum = (t[None, :] <= t[:, None])
    mats = [cum]
    for m in HG_LEVELS:
        idx = (t // (2 * m)) * (2 * m) + m - (1 if reverse else 0)
        mats.append(cum[idx])
    stack = np.concatenate(mats, axis=0).astype(np.float32)
    x = t[:, None] ^ t[None, :]
    lvl = np.zeros((c, c), np.int32)
    for li, m in enumerate(HG_LEVELS):
        lvl = np.where((x >= m) & (x < 2 * m), li + 1, lvl)
    visible = (t[None, :] >= t[:, None]) if reverse else (t[None, :] <= t[:, None])
    lvl = np.where(visible, lvl, -1).astype(np.int32)
    return stack, lvl


def _hgrn_kernel(q_ref, f_ref, v_ref, lb_ref, stack_ref, lvl_ref, o_ref, st_ref, *, reverse):
    c = HG_CHUNK

    @pl.when(pl.program_id(1) == 0)
    def _():
        st_ref[...] = jnp.zeros_like(st_ref)

    stack = stack_ref[...]
    lvl = lvl_ref[...]
    for h in range(HG_HEADS):
        sl = slice(h * HEAD_W, (h + 1) * HEAD_W)
        q = _silu(q_ref[0, :, sl])
        fr = f_ref[0, :, sl]
        v = v_ref[0, :, sl].astype(BF16)
        log_lb = lb_ref[0:1, sl]
        log1m_lb = lb_ref[1:2, sl]
        c2 = log1m_lb + jnp.minimum(fr, 0.0) - jnp.log1p(jnp.exp(-jnp.abs(fr)))
        g = jnp.maximum(log_lb, c2) + jnp.log1p(jnp.exp(-jnp.abs(log_lb - c2)))
        k = 1.0 - jnp.exp(g)
        g1 = g.astype(BF16)
        r1 = g - g1.astype(F32)
        g2 = r1.astype(BF16)
        g3 = (r1 - g2.astype(F32)).astype(BF16)
        bb = _dot(stack, g1) + _dot(stack, g2) + _dot(stack, g3)
        b = bb[0:c]
        a = jnp.where(lvl == 0, _dot_nt(q.astype(BF16), k.astype(BF16)), 0.0)
        for li in range(len(HG_LEVELS)):
            e = jnp.exp(-jnp.abs(b - bb[(li + 1) * c:(li + 2) * c]))
            a = jnp.where(lvl == li + 1, _dot_nt((q * e).astype(BF16), (k * e).astype(BF16)), a)
        st = st_ref[h]
        o = _dot(a.astype(BF16), v) + _dot_nt((q * jnp.exp(b)).astype(BF16), st.astype(BF16))
        o_ref[0, :, sl] = o
        b_end = b[0:1] if reverse else b[c - 1:c]
        k_end = (k * jnp.exp(b_end - b)).astype(BF16)
        st_ref[h] = st * jnp.exp(b_end) + _dot_tn(v, k_end)


def _hgrn(p, lb2, f_col, reverse, n_ctx):
    bsz, t, _ = p.shape
    c = HG_CHUNK
    nc, nt = n_ctx // c, t // c
    stack, lvl = _hgrn_tables(reverse)
    if reverse:
        def row(j):
            return jnp.where(j < nc, nc - 1 - j, nc + nt - 1 - j)
    else:
        def row(j):
            return j
    nst = stack.shape[0]
    return pl.pallas_call(
        functools.partial(_hgrn_kernel, reverse=reverse),
        grid=(bsz, nt),
        in_specs=[pl.BlockSpec((1, c, BRANCH_W), lambda b, j: (b, row(j), COL_Q)),
                  pl.BlockSpec((1, c, BRANCH_W), lambda b, j: (b, row(j), f_col)),
                  pl.BlockSpec((1, c, BRANCH_W), lambda b, j: (b, row(j), COL_I)),
                  pl.BlockSpec((2, BRANCH_W), lambda b, j: (0, 0)),
                  pl.BlockSpec((nst, c), lambda b, j: (0, 0)),
                  pl.BlockSpec((c, c), lambda b, j: (0, 0))],
        out_specs=pl.BlockSpec((1, c, BRANCH_W), lambda b, j: (b, row(j), 0)),
        out_shape=jax.ShapeDtypeStruct((bsz, t, BRANCH_W), F32),
        scratch_shapes=[pltpu.VMEM((HG_HEADS, HEAD_W, HEAD_W), F32)],
        compiler_params=_cparams(("parallel", "arbitrary")),
        name="hgrn_bwd" if reverse else "hgrn_fwd",
    )(p, p, p, lb2, jnp.asarray(stack, BF16), jnp.asarray(lvl))


def _attn_prep_kernel(q_ref, k_ref, v_ref, cos_ref, sin_ref, qo_ref, ko_ref, vo_ref):
    cos = cos_ref[...]
    sin = sin_ref[...]
    lane = lax.broadcasted_iota(jnp.int32, cos.shape, 1)
    first_half = (lane % DA_DH) < (DA_DH // 2)

    def rope(x):
        up = pltpu.roll(x, HEAD_W - DA_DH // 2, 1)
        down = pltpu.roll(x, DA_DH // 2, 1)
        return x * cos + jnp.where(first_half, up, down) * sin

    for h in range(DA_HEADS):
        sl = slice(h * HEAD_W, (h + 1) * HEAD_W)
        qo_ref[0, :, sl] = (rope(q_ref[0, :, sl]) * (DA_DH ** -0.5)).astype(BF16)
        ko_ref[0, :, sl] = rope(k_ref[0, :, sl]).astype(BF16)
    vo_ref[0] = v_ref[0].astype(BF16)


def _attn_prep(p, cos, sin):
    bsz, t, _ = p.shape
    spec = lambda col: pl.BlockSpec((1, ROW_TILE, BRANCH_W), lambda b, i: (b, i, col))
    tab = pl.BlockSpec((ROW_TILE, HEAD_W), lambda b, i: (i, 0))
    out = jax.ShapeDtypeStruct((bsz, t, BRANCH_W), BF16)
    return pl.pallas_call(
        _attn_prep_kernel,
        grid=(bsz, t // ROW_TILE),
        in_specs=[spec(COL_QA), spec(COL_KA), spec(COL_VA), tab, tab],
        out_specs=[spec(0), spec(0), spec(0)],
        out_shape=[out, out, out],
        compiler_params=_cparams(("parallel", "parallel")),
        name="attn_prep",
    )(p, p, p, cos, sin)


def _attn_kernel(q_ref, k_ref, v_ref, lam_ref, o_ref, *, n_ctx, n_ctx_tiles):
    lam = lam_ref[0:1, 0:1]

    def attend(n_keys):
        q = q_ref[0]
        k = k_ref[0, 0:n_keys, :]
        v = v_ref[0, 0:n_keys, :]
        lane = lax.broadcasted_iota(jnp.int32, q.shape, 1)
        zero = jnp.zeros_like(q)
        outs = []
        for first in (True, False):
            qm = jnp.where((lane < DA_DH) == first, q, zero)
            s = _dot_nt(qm, k)
            e = jnp.exp(s - jnp.max(s, axis=-1, keepdims=True))
            outs.append(_dot(e.astype(BF16), v) / jnp.sum(e, axis=-1, keepdims=True))
        o_ref[0] = outs[0] - lam * outs[1]

    is_ctx = pl.program_id(2) < n_ctx_tiles

    @pl.when(is_ctx)
    def _():
        attend(n_ctx)

    @pl.when(jnp.logical_not(is_ctx))
    def _():
        attend(k_ref.shape[1])


def _attention(q, k, v, lam, n_ctx):
    bsz, t, _ = q.shape
    n_ctx_tiles = n_ctx // ROW_TILE
    return pl.pallas_call(
        functools.partial(_attn_kernel, n_ctx=n_ctx, n_ctx_tiles=n_ctx_tiles),
        grid=(bsz, DA_HEADS, t // ROW_TILE),
        in_specs=[pl.BlockSpec((1, ROW_TILE, HEAD_W), lambda b, h, i: (b, i, h)),
                  pl.BlockSpec((1, t, HEAD_W), lambda b, h, i: (b, 0, h)),
                  pl.BlockSpec((1, t, HEAD_W), lambda b, h, i: (b, 0, h)),
                  pl.BlockSpec((8, HEAD_W), lambda b, h, i: (0, 0))],
        out_specs=pl.BlockSpec((1, ROW_TILE, HEAD_W), lambda b, h, i: (b, i, h)),
        out_shape=jax.ShapeDtypeStruct((bsz, t, BRANCH_W), F32),
        compiler_params=_cparams(("parallel", "parallel", "parallel")),
        name="diff_attn",
    )(q, k, v, lam)


def _dft_tables(n):
    idx = np.arange(n, dtype=np.int64)
    ang = 2.0 * np.pi * ((idx[:, None] * idx[None, :]) % n).astype(np.float64) / n
    scale = 1.0 / math.sqrt(n * FR_DG)
    return np.concatenate([np.cos(ang), -np.sin(ang)], axis=1).astype(np.float32) * np.float32(scale)


def _channel_dft_tables():
    idx = np.arange(FR_DG, dtype=np.int64)
    ang = 2.0 * np.pi * ((idx[:, None] * idx[None, :]) % FR_DG).astype(np.float64) / FR_DG
    eye = np.eye(FR_GROUPS)
    return (np.kron(eye, np.cos(ang)).astype(np.float32), np.kron(eye, np.sin(ang)).astype(np.float32))


def _fourier_kernel(u_ref, cc_ref, sc_ref, wc_ref, wl_ref, o_ref, zc_ref, zl_ref, *, n_ctx, n_ctx_tiles):
    i = pl.program_id(1)
    n_lat = u_ref.shape[1] - n_ctx

    @pl.when(i == 0)
    def _():
        u = u_ref[0].astype(BF16)
        zcos = _dot(u, cc_ref[...]).astype(BF16)
        zsin = _dot(u, sc_ref[...]).astype(BF16)
        zc_ref[0:n_ctx] = zcos[0:n_ctx]
        zc_ref[n_ctx:2 * n_ctx] = zsin[0:n_ctx]
        zl_ref[0:n_lat] = zcos[n_ctx:]
        zl_ref[n_lat:2 * n_lat] = zsin[n_ctx:]

    @pl.when(i < n_ctx_tiles)
    def _():
        o_ref[0] = _dot(wc_ref[...], zc_ref[...])

    @pl.when(i >= n_ctx_tiles)
    def _():
        o_ref[0] = _dot(wl_ref[...], zl_ref[...])


def _fourier(p, n_ctx):
    bsz, t, _ = p.shape
    n_lat = t - n_ctx
    n_ctx_tiles = n_ctx // ROW_TILE
    cc, sc = _channel_dft_tables()
    wc = jnp.asarray(_dft_tables(n_ctx)).astype(BF16)
    wl = jnp.asarray(_dft_tables(n_lat)).astype(BF16)
    cc = jnp.asarray(cc).astype(BF16)
    sc = jnp.asarray(sc).astype(BF16)
    return pl.pallas_call(
        functools.partial(_fourier_kernel, n_ctx=n_ctx, n_ctx_tiles=n_ctx_tiles),
        grid=(bsz, t // ROW_TILE),
        in_specs=[pl.BlockSpec((1, t, BRANCH_W), lambda b, i: (b, 0, COL_U)),
                  pl.BlockSpec((BRANCH_W, BRANCH_W), lambda b, i: (0, 0)),
                  pl.BlockSpec((BRANCH_W, BRANCH_W), lambda b, i: (0, 0)),
                  pl.BlockSpec((ROW_TILE, 2 * n_ctx), lambda b, i: (jnp.minimum(i, n_ctx_tiles - 1), 0)),
                  pl.BlockSpec((ROW_TILE, 2 * n_lat), lambda b, i: (jnp.maximum(i - n_ctx_tiles, 0), 0))],
        out_specs=pl.BlockSpec((1, ROW_TILE, BRANCH_W), lambda b, i: (b, i, 0)),
        out_shape=jax.ShapeDtypeStruct((bsz, t, BRANCH_W), F32),
        scratch_shapes=[pltpu.VMEM((2 * n_ctx, BRANCH_W), BF16), pltpu.VMEM((2 * n_lat, BRANCH_W), BF16)],
        compiler_params=_cparams(("parallel", "arbitrary")),
        name="fourier",
    )(p, cc, sc, wc, wl)


def _head_rms(x, gain):
    parts = []
    for h in range(BRANCH_W // HEAD_W):
        xh = x[:, h * HEAD_W:(h + 1) * HEAD_W]
        parts.append(xh * lax.rsqrt(jnp.mean(xh * xh, axis=-1, keepdims=True) + LN_EPS) * gain)
    return jnp.concatenate(parts, axis=-1)


def _merge_kernel(x_ref, mod_ref, gates_ref, og_ref, of_ref, ob_ref, at_ref, fr_ref,
                  hg_ref, da_ref, wa_ref, wb_ref, wc_ref, wo_ref, ln_ref, wr_ref,
                  x1_ref, h2_ref, lg_ref, *, alpha, attn_scale):
    d = x_ref.shape[2]
    a = _head_rms(of_ref[0] + ob_ref[0], hg_ref[...]) * _silu(og_ref[0])
    bq = _head_rms(at_ref[0], da_ref[...]) * attn_scale
    g = jax.nn.sigmoid(gates_ref[0])
    m = (g[:, 0:d] * _dot(a.astype(BF16), wa_ref[...])
         + g[:, d:2 * d] * _dot(bq.astype(BF16), wb_ref[...])
         + g[:, 2 * d:3 * d] * _dot(fr_ref[0].astype(BF16), wc_ref[...]))
    mix = _dot(m.astype(BF16), wo_ref[...])
    gate1 = mod_ref[0, 0, 2:3, :]
    shift2 = mod_ref[0, 0, 3:4, :]
    scale2 = mod_ref[0, 0, 4:5, :]
    x1 = _ln(alpha * x_ref[0] + gate1 * mix) * ln_ref[0:1, :] + ln_ref[1:2, :]
    x1_ref[0] = x1
    h2 = _ln(x1) * (1.0 + scale2) + shift2
    h2_ref[0] = h2.astype(BF16)
    lg_ref[...] = lax.dot_general(wr_ref[...], h2, (((1,), (1,)), ((), ())),
                                  preferred_element_type=F32, precision=lax.Precision.HIGHEST)


def _merge(x, mods, p, o_f, o_b, at, fr, hg_g, da_g, wa, wb, wc, wo, ln1, wr_t, alpha, attn_scale, n_ctx_tiles):
    bsz, t, d = x.shape
    tiles = t // ROW_TILE
    row = lambda w, col: pl.BlockSpec((1, ROW_TILE, w), lambda b, i: (b, i, col))
    full = lambda arr: pl.BlockSpec(arr.shape, lambda b, i: (0,) * arr.ndim)
    return pl.pallas_call(
        functools.partial(_merge_kernel, alpha=alpha, attn_scale=attn_scale),
        grid=(bsz, tiles),
        in_specs=[row(d, 0),
                  pl.BlockSpec((1, 1, 6, d), lambda b, i: (b, (i >= n_ctx_tiles).astype(jnp.int32), 0, 0)),
                  row(GATES_BLOCKS * BRANCH_W, 0), row(BRANCH_W, COL_OG),
                  row(BRANCH_W, 0), row(BRANCH_W, 0), row(BRANCH_W, 0), row(BRANCH_W, 0),
                  full(hg_g), full(da_g), full(wa), full(wb), full(wc), full(wo), full(ln1), full(wr_t)],
        out_specs=[row(d, 0), row(d, 0),
                   pl.BlockSpec((N_EXPERTS, ROW_TILE), lambda b, i: (0, b * tiles + i))],
        out_shape=[jax.ShapeDtypeStruct((bsz, t, d), F32),
                   jax.ShapeDtypeStruct((bsz, t, d), BF16),
                   jax.ShapeDtypeStruct((N_EXPERTS, bsz * t), F32)],
        compiler_params=_cparams(("parallel", "parallel")),
        name="merge",
    )(x, mods, p, p, o_f, o_b, at, fr, hg_g, da_g, wa, wb, wc, wo, ln1, wr_t)


def _router_kernel(lg_ref, bias_ref, w_ref):
    per = N_EXPERTS // N_GROUPS
    scores = [jax.nn.sigmoid(lg_ref[j * N_GROUPS:(j + 1) * N_GROUPS, :]) for j in range(per)]
    sel = [scores[j] + bias_ref[j * N_GROUPS:(j + 1) * N_GROUPS, :] for j in range(per)]
    shape = sel[0].shape
    grp = lax.broadcasted_iota(jnp.int32, shape, 0)

    def beats(other, me, tie):
        if isinstance(tie, bool):
            return jnp.where((other >= me) if tie else (other > me), 1, 0)
        return jnp.where(other > me, 1, jnp.where(other == me, tie, 0))

    m1 = jnp.maximum(sel[0], sel[1])
    m2 = jnp.minimum(sel[0], sel[1])
    for j in range(2, per):
        m2 = jnp.maximum(m2, jnp.minimum(m1, sel[j]))
        m1 = jnp.maximum(m1, sel[j])
    gscore = m1 + m2
    grank = jnp.zeros(shape, jnp.int32)
    lower_group = [None] + [jnp.where(grp >= kk, 1, 0) for kk in range(1, N_GROUPS)]
    for kk in range(1, N_GROUPS):
        other = pltpu.roll(gscore, kk, 0)
        grank = grank + beats(other, gscore, lower_group[kk])
    gmask = grank < TOPK_GROUPS
    sel = [jnp.where(gmask, s, MASK_VALUE) for s in sel]
    rolled = [[None] + [pltpu.roll(s, kk, 0) for kk in range(1, N_GROUPS)] for s in sel]
    weights = []
    for j in range(per):
        rank = jnp.zeros(shape, jnp.int32)
        for j2 in range(per):
            for kk in range(N_GROUPS):
                if kk == 0 and j2 == j:
                    continue
                if kk == 0:
                    rank = rank + beats(sel[j2], sel[j], j2 < j)
                else:
                    rank = rank + beats(rolled[j2][kk], sel[j], lower_group[kk])
        weights.append(jnp.where(rank < TOP_K, scores[j], 0.0))
    total = weights[0]
    for j in range(1, per):
        total = total + weights[j]
    total = jnp.sum(total, axis=0, keepdims=True)
    for j in range(per):
        w_ref[j * N_GROUPS:(j + 1) * N_GROUPS, :] = weights[j] / total * ROUTED_SCALE


def _router(logits_t, bias_col):
    e, n = logits_t.shape
    return pl.pallas_call(
        _router_kernel,
        grid=(n // ROUTER_TILE,),
        in_specs=[pl.BlockSpec((e, ROUTER_TILE), lambda i: (0, i)),
                  pl.BlockSpec((e, ROUTER_TILE), lambda i: (0, 0))],
        out_specs=pl.BlockSpec((e, ROUTER_TILE), lambda i: (0, i)),
        out_shape=jax.ShapeDtypeStruct((e, n), F32),
        compiler_params=_cparams(("parallel",)),
        name="router",
    )(logits_t, bias_col)


def _moe_kernel(h_ref, w_ref, wg_ref, wu_ref, wd_ref, y_ref):
    e = pl.program_id(1)
    x = h_ref[...]
    hid = _silu(_dot(x, wg_ref[0])) * _dot(x, wu_ref[0])
    wts = w_ref[...]
    lane = lax.broadcasted_iota(jnp.int32, wts.shape, 1)
    wcol = jnp.sum(jnp.where(lane == e, wts, 0.0), axis=1, keepdims=True)
    contrib = _dot((hid * wcol).astype(BF16), wd_ref[0])

    @pl.when(e == 0)
    def _():
        y_ref[...] = contrib

    @pl.when(e > 0)
    def _():
        y_ref[...] += contrib


def _moe(h2, wts, wg, wu, wd):
    n, d = h2.shape
    ne = wg.shape[0]
    return pl.pallas_call(
        _moe_kernel,
        grid=(n // MOE_TILE, ne),
        in_specs=[pl.BlockSpec((MOE_TILE, d), lambda i, e: (i, 0)),
                  pl.BlockSpec((MOE_TILE, wts.shape[1]), lambda i, e: (i, 0)),
                  pl.BlockSpec((1, d, EXPERT_DIM), lambda i, e: (e, 0, 0)),
                  pl.BlockSpec((1, d, EXPERT_DIM), lambda i, e: (e, 0, 0)),
                  pl.BlockSpec((1, EXPERT_DIM, d), lambda i, e: (e, 0, 0))],
        out_specs=pl.BlockSpec((MOE_TILE, d), lambda i, e: (i, 0)),
        out_shape=jax.ShapeDtypeStruct((n, d), F32),
        compiler_params=_cparams(("parallel", "arbitrary")),
        name="experts",
    )(h2, wts, wg, wu, wd)


def _final_kernel(x_ref, y_ref, mod_ref, ln_ref, o_ref, *, alpha):
    gate2 = mod_ref[0, 0, 5:6, :]
    o_ref[0] = _ln(alpha * x_ref[0] + gate2 * y_ref[0]) * ln_ref[0:1, :] + ln_ref[1:2, :]


def _final(x1, y, mods, ln2, alpha, n_ctx_tiles):
    bsz, t, d = x1.shape
    row = pl.BlockSpec((1, ROW_TILE, d), lambda b, i: (b, i, 0))
    return pl.pallas_call(
        functools.partial(_final_kernel, alpha=alpha),
        grid=(bsz, t // ROW_TILE),
        in_specs=[row, row,
                  pl.BlockSpec((1, 1, 6, d), lambda b, i: (b, (i >= n_ctx_tiles).astype(jnp.int32), 0, 0)),
                  pl.BlockSpec(ln2.shape, lambda b, i: (0, 0))],
        out_specs=row,
        out_shape=jax.ShapeDtypeStruct((bsz, t, d), F32),
        compiler_params=_cparams(("parallel", "parallel")),
        name="final_norm",
    )(x1, y, mods, ln2)


def _rope_tables(n_lat, n_ctx):
    rows = n_lat // GRID_W
    row = jnp.repeat(jnp.arange(rows, dtype=F32), GRID_W)
    col = (jnp.arange(n_lat) % GRID_W).astype(F32)
    n_freq = DA_DH // 4
    inv = ROPE_THETA ** (-jnp.arange(n_freq, dtype=F32) / n_freq)
    ang = jnp.concatenate([row[:, None] * inv, col[:, None] * inv], axis=-1)
    ang = jnp.concatenate([ang, ang], axis=-1)
    cos, sin = jnp.cos(ang), jnp.sin(ang)
    sign = jnp.where(jnp.arange(DA_DH) < DA_DH // 2, -1.0, 1.0).astype(F32)
    cos = jnp.tile(cos, (1, HEAD_W // DA_DH))
    sin = jnp.tile(sin * sign, (1, HEAD_W // DA_DH))
    cos = jnp.concatenate([jnp.ones((n_ctx, HEAD_W), F32), cos], axis=0)
    sin = jnp.concatenate([jnp.zeros((n_ctx, HEAD_W), F32), sin], axis=0)
    return cos, sin


def kernel(x, c, ctx, c_ctx, w_ada, b_ada, w_in, lb_logits, hg_norm_g, lam_params, da_norm_g, w_branch_a, w_branch_b, w_branch_c, w_out, ln1_g, ln1_b, ln2_g, ln2_b, w_router, router_bias, w_exp_gate, w_exp_up, w_exp_down, w_sh_gate, w_sh_up, w_sh_down):
    bsz, n_lat, d = x.shape
    n_ctx = ctx.shape[1]
    depth = w_in.shape[0]
    t = n_ctx + n_lat
    assert n_ctx % ROW_TILE == 0 and n_lat % ROW_TILE == 0 and (bsz * t) % MOE_TILE == 0
    assert n_ctx % HG_CHUNK == 0 and n_lat % HG_CHUNK == 0 and n_lat % GRID_W == 0
    n_ctx_tiles = n_ctx // ROW_TILE
    alpha = (2 * depth) ** 0.25
    per = N_EXPERTS // N_GROUPS

    cos, sin = _rope_tables(n_lat, n_ctx)
    sm = jax.nn.softmax(lb_logits.astype(F32), axis=0)
    lower = jnp.cumsum(sm, axis=0) - sm[0:1]
    log_lb = jnp.log(jnp.maximum(lower, LB_FLOOR))
    log1m_lb = jnp.log1p(-lower)

    n_rows = -(-(bsz + 1) // 8) * 8
    cvec = jnp.concatenate([c, c_ctx[None, :], jnp.zeros((n_rows - bsz - 1, d), F32)], axis=0)

    n_branch_cols = w_in.shape[2] - 3 * d

    def member_major(a):
        return a.reshape((N_GROUPS, per) + a.shape[1:]).swapaxes(0, 1).reshape(a.shape)

    xs = jnp.concatenate([ctx, x], axis=1)
    for l in range(depth):
        mod = _ada(cvec, w_ada[l], b_ada[l])
        mod_lat = mod[:bsz].reshape(bsz, 1, 6, d)
        mod_ctx = jnp.broadcast_to(mod[bsz].reshape(1, 1, 6, d), (bsz, 1, 6, d))
        mods = jnp.concatenate([mod_ctx, mod_lat], axis=1)

        w_in_l = jnp.concatenate([w_in[l][:, n_branch_cols:], w_in[l][:, :n_branch_cols]], axis=1)
        p = _inproj(xs, mods, w_in_l.astype(BF16), n_ctx_tiles)
        lb_f = jnp.stack([log_lb[l, 0], log1m_lb[l, 0]])
        lb_b = jnp.stack([log_lb[l, 1], log1m_lb[l, 1]])
        o_f = _hgrn(p, lb_f, COL_FF, False, n_ctx)
        o_b = _hgrn(p, lb_b, COL_FB, True, n_ctx)

        lam_init = 0.8 - 0.6 * math.exp(-0.3 * l)
        lp = lam_params[l].astype(F32)
        lam = jnp.exp(jnp.sum(lp[0] * lp[1])) - jnp.exp(jnp.sum(lp[2] * lp[3])) + lam_init
        qr, kr, vr = _attn_prep(p, cos, sin)
        at = _attention(qr, kr, vr, jnp.full((8, HEAD_W), lam, F32), n_ctx)
        fr = _fourier(p, n_ctx)

        ln1 = jnp.stack([ln1_g[l], ln1_b[l]])
        hg_g = hg_norm_g[l].reshape(1, HEAD_W)
        da_g = da_norm_g[l].reshape(1, HEAD_W)
        x1, h2, logits_t = _merge(
            xs, mods, p, o_f, o_b, at, fr, hg_g, da_g,
            w_branch_a[l].astype(BF16), w_branch_b[l].astype(BF16), w_branch_c[l].astype(BF16),
            w_out[l].astype(BF16), ln1, member_major(w_router[l].T), alpha, 1.0 - lam_init, n_ctx_tiles)

        bias_col = jnp.broadcast_to(member_major(router_bias[l])[:, None], (N_EXPERTS, ROUTER_TILE))
        wts_t = _router(logits_t, bias_col)
        n_tok = bsz * t
        wts = jnp.concatenate([wts_t.T, jnp.ones((n_tok, 1), F32),
                               jnp.zeros((n_tok, HEAD_W - N_EXPERTS - 1), F32)], axis=1)
        wg = jnp.concatenate([member_major(w_exp_gate[l]), w_sh_gate[l][None]], axis=0).astype(BF16)
        wu = jnp.concatenate([member_major(w_exp_up[l]), w_sh_up[l][None]], axis=0).astype(BF16)
        wd = jnp.concatenate([member_major(w_exp_down[l]), w_sh_down[l][None]], axis=0).astype(BF16)
        y = _moe(h2.reshape(n_tok, d), wts, wg, wu, wd).reshape(bsz, t, d)

        ln2 = jnp.stack([ln2_g[l], ln2_b[l]])
        xs = _final(x1, y, mods, ln2, alpha, n_ctx_tiles)
    return xs[:, n_ctx:]
```

```python
import functools
import math

import numpy as np
import jax
import jax.numpy as jnp
from jax import lax
from jax.experimental import pallas as pl
from jax.experimental.pallas import tpu as pltpu

F32 = jnp.float32
BF16 = jnp.bfloat16

HG_HEADS = 4
HG_DK = 128
LB_FLOOR = 1e-30
DA_HEADS = 4
DA_DH = 64
ROPE_THETA = 10000.0
GRID_W = 64
FR_GROUPS = 4
FR_DG = 128
N_EXPERTS = 64
TOP_K = 8
EXPERT_DIM = 256
N_GROUPS = 8
TOPK_GROUPS = 4
ROUTED_SCALE = 2.5
MASK_VALUE = -1e9
LN_EPS = 1e-5

HEAD_W = 128
BRANCH_W = 512
GATES_BLOCKS = 6
COL_Q, COL_FF, COL_FB, COL_I, COL_OG, COL_QA, COL_KA, COL_VA, COL_U = range(6, 15)

ROW_TILE = 256
HG_CHUNK = 128
HG_LEVELS = (64, 32, 16, 8, 4, 2, 1)
IN_N_TILE = 2560
MOE_TILE = 1024
ROUTER_TILE = 512
VMEM_LIMIT = 48 * 1024 * 1024


def _cparams(sem):
    return pltpu.CompilerParams(dimension_semantics=sem, vmem_limit_bytes=VMEM_LIMIT)


def _dot(a, b):
    return jnp.dot(a, b, preferred_element_type=F32)


def _dot_nt(a, b):
    return lax.dot_general(a, b, (((1,), (1,)), ((), ())), preferred_element_type=F32)


def _dot_tn(a, b):
    return lax.dot_general(a, b, (((0,), (0,)), ((), ())), preferred_element_type=F32)


def _silu(x):
    return x * jax.nn.sigmoid(x)


def _ln(x):
    mu = jnp.mean(x, axis=-1, keepdims=True)
    xc = x - mu
    var = jnp.mean(xc * xc, axis=-1, keepdims=True)
    return xc * lax.rsqrt(var + LN_EPS)


def _ada_kernel(c_ref, w_ref, b_ref, o_ref):
    o_ref[...] = jnp.dot(_silu(c_ref[...]), w_ref[...], preferred_element_type=F32,
                         precision=lax.Precision.HIGHEST) + b_ref[...]


def _ada(cvec, w, b):
    rows, d = cvec.shape
    n = w.shape[1]
    tn = 1536
    return pl.pallas_call(
        _ada_kernel,
        grid=(n // tn,),
        in_specs=[pl.BlockSpec((rows, d), lambda j: (0, 0)),
                  pl.BlockSpec((d, tn), lambda j: (0, j)),
                  pl.BlockSpec((1, tn), lambda j: (0, j))],
        out_specs=pl.BlockSpec((rows, tn), lambda j: (0, j)),
        out_shape=jax.ShapeDtypeStruct((rows, n), F32),
        compiler_params=_cparams(("parallel",)),
        name="ada",
    )(cvec, w, b.reshape(1, n))


def _inproj_kernel(x_ref, mod_ref, w_ref, o_ref):
    shift = mod_ref[0, 0, 0:1, :]
    scale = mod_ref[0, 0, 1:2, :]
    h = _ln(x_ref[0]) * (1.0 + scale) + shift
    o_ref[0] = _dot(h.astype(BF16), w_ref[...])


def _inproj(x, mods, w, n_ctx_tiles):
    bsz, t, d = x.shape
    n = w.shape[1]
    return pl.pallas_call(
        _inproj_kernel,
        grid=(n // IN_N_TILE, bsz, t // ROW_TILE),
        in_specs=[pl.BlockSpec((1, ROW_TILE, d), lambda j, b, i: (b, i, 0)),
                  pl.BlockSpec((1, 1, 6, d), lambda j, b, i: (b, (i >= n_ctx_tiles).astype(jnp.int32), 0, 0)),
                  pl.BlockSpec((d, IN_N_TILE), lambda j, b, i: (0, j))],
        out_specs=pl.BlockSpec((1, ROW_TILE, IN_N_TILE), lambda j, b, i: (b, i, j)),
        out_shape=jax.ShapeDtypeStruct((bsz, t, n), F32),
        compiler_params=_cparams(("parallel", "parallel", "parallel")),
        name="inproj",
    )(x, mods, w)


def _hgrn_tables(reverse):
    c = HG_CHUNK
    t = np.arange(c)
    if reverse:
        cum = (t[None, :] >= t[:, None])
    else:
        cum = (t[None, :] <= t[:, None])
    mats = [cum]
    for m in HG_LEVELS:
        idx = (t // (2 * m)) * (2 * m) + m - (1 if reverse else 0)
        mats.append(cum[idx])
    stack = np.concatenate(mats, axis=0).astype(np.float32)
    x = t[:, None] ^ t[None, :]
    lvl = np.zeros((c, c), np.int32)
    for li, m in enumerate(HG_LEVELS):
        lvl = np.where((x >= m) & (x < 2 * m), li + 1, lvl)
    visible = (t[None, :] >= t[:, None]) if reverse else (t[None, :] <= t[:, None])
    lvl = np.where(visible, lvl, -1).astype(np.int32)
    return stack, lvl


def _hgrn_kernel(q_ref, f_ref, v_ref, lb_ref, stack_ref, lvl_ref, o_ref, st_ref, *, reverse):
    c = HG_CHUNK

    @pl.when(pl.program_id(1) == 0)
    def _():
        st_ref[...] = jnp.zeros_like(st_ref)

    stack = stack_ref[...]
    lvl = lvl_ref[...]
    for h in range(HG_HEADS):
        sl = slice(h * HEAD_W, (h + 1) * HEAD_W)
        q = _silu(q_ref[0, :, sl])
        fr = f_ref[0, :, sl]
        v = v_ref[0, :, sl].astype(BF16)
        log_lb = lb_ref[0:1, sl]
        log1m_lb = lb_ref[1:2, sl]
        c2 = log1m_lb + jnp.minimum(fr, 0.0) - jnp.log1p(jnp.exp(-jnp.abs(fr)))
        g = jnp.maximum(log_lb, c2) + jnp.log1p(jnp.exp(-jnp.abs(log_lb - c2)))
        k = 1.0 - jnp.exp(g)
        g1 = g.astype(BF16)
        r1 = g - g1.astype(F32)
        g2 = r1.astype(BF16)
        g3 = (r1 - g2.astype(F32)).astype(BF16)
        bb = _dot(stack, g1) + _dot(stack, g2) + _dot(stack, g3)
        b = bb[0:c]
        a = jnp.where(lvl == 0, _dot_nt(q.astype(BF16), k.astype(BF16)), 0.0)
        for li in range(len(HG_LEVELS)):
            e = jnp.exp(-jnp.abs(b - bb[(li + 1) * c:(li + 2) * c]))
            a = jnp.where(lvl == li + 1, _dot_nt((q * e).astype(BF16), (k * e).astype(BF16)), a)
        st = st_ref[h]
        o = _dot(a.astype(BF16), v) + _dot_nt((q * jnp.exp(b)).astype(BF16), st.astype(BF16))
        o_ref[0, :, sl] = o
        b_end = b[0:1] if reverse else b[c - 1:c]
        k_end = (k * jnp.exp(b_end - b)).astype(BF16)
        st_ref[h] = st * jnp.exp(b_end) + _dot_tn(v, k_end)


def _hgrn(p, lb2, f_col, reverse, n_ctx):
    bsz, t, _ = p.shape
    c = HG_CHUNK
    nc, nt = n_ctx // c, t // c
    stack, lvl = _hgrn_tables(reverse)
    if reverse:
        def row(j):
            return jnp.where(j < nc, nc - 1 - j, nc + nt - 1 - j)
    else:
        def row(j):
            return j
    nst = stack.shape[0]
    return pl.pallas_call(
        functools.partial(_hgrn_kernel, reverse=reverse),
        grid=(bsz, nt),
        in_specs=[pl.BlockSpec((1, c, BRANCH_W), lambda b, j: (b, row(j), COL_Q)),
                  pl.BlockSpec((1, c, BRANCH_W), lambda b, j: (b, row(j), f_col)),
                  pl.BlockSpec((1, c, BRANCH_W), lambda b, j: (b, row(j), COL_I)),
                  pl.BlockSpec((2, BRANCH_W), lambda b, j: (0, 0)),
                  pl.BlockSpec((nst, c), lambda b, j: (0, 0)),
                  pl.BlockSpec((c, c), lambda b, j: (0, 0))],
        out_specs=pl.BlockSpec((1, c, BRANCH_W), lambda b, j: (b, row(j), 0)),
        out_shape=jax.ShapeDtypeStruct((bsz, t, BRANCH_W), F32),
        scratch_shapes=[pltpu.VMEM((HG_HEADS, HEAD_W, HEAD_W), F32)],
        compiler_params=_cparams(("parallel", "arbitrary")),
        name="hgrn_bwd" if reverse else "hgrn_fwd",
    )(p, p, p, lb2, jnp.asarray(stack, BF16), jnp.asarray(lvl))


def _attn_prep_kernel(q_ref, k_ref, v_ref, cos_ref, sin_ref, qo_ref, ko_ref, vo_ref):
    cos = cos_ref[...]
    sin = sin_ref[...]
    lane = lax.broadcasted_iota(jnp.int32, cos.shape, 1)
    first_half = (lane % DA_DH) < (DA_DH // 2)

    def rope(x):
        up = pltpu.roll(x, HEAD_W - DA_DH // 2, 1)
        down = pltpu.roll(x, DA_DH // 2, 1)
        return x * cos + jnp.where(first_half, up, down) * sin

    for h in range(DA_HEADS):
        sl = slice(h * HEAD_W, (h + 1) * HEAD_W)
        qo_ref[0, :, sl] = (rope(q_ref[0, :, sl]) * (DA_DH ** -0.5)).astype(BF16)
        ko_ref[0, :, sl] = rope(k_ref[0, :, sl]).astype(BF16)
    vo_ref[0] = v_ref[0].astype(BF16)


def _attn_prep(p, cos, sin):
    bsz, t, _ = p.shape
    spec = lambda col: pl.BlockSpec((1, ROW_TILE, BRANCH_W), lambda b, i: (b, i, col))
    tab = pl.BlockSpec((ROW_TILE, HEAD_W), lambda b, i: (i, 0))
    out = jax.ShapeDtypeStruct((bsz, t, BRANCH_W), BF16)
    return pl.pallas_call(
        _attn_prep_kernel,
        grid=(bsz, t // ROW_TILE),
        in_specs=[spec(COL_QA), spec(COL_KA), spec(COL_VA), tab, tab],
        out_specs=[spec(0), spec(0), spec(0)],
        out_shape=[out, out, out],
        compiler_params=_cparams(("parallel", "parallel")),
        name="attn_prep",
    )(p, p, p, cos, sin)


def _attn_kernel(q_ref, k_ref, v_ref, lam_ref, o_ref, *, n_ctx, n_ctx_tiles):
    lam = lam_ref[0:1, 0:1]

    def attend(n_keys):
        q = q_ref[0]
        k = k_ref[0, 0:n_keys, :]
        v = v_ref[0, 0:n_keys, :]
        lane = lax.broadcasted_iota(jnp.int32, q.shape, 1)
        zero = jnp.zeros_like(q)
        outs = []
        for first in (True, False):
            qm = jnp.where((lane < DA_DH) == first, q, zero)
            s = _dot_nt(qm, k)
            e = jnp.exp(s - jnp.max(s, axis=-1, keepdims=True))
            outs.append(_dot(e.astype(BF16), v) / jnp.sum(e, axis=-1, keepdims=True))
        o_ref[0] = outs[0] - lam * outs[1]

    is_ctx = pl.program_id(2) < n_ctx_tiles

    @pl.when(is_ctx)
    def _():
        attend(n_ctx)

    @pl.when(jnp.logical_not(is_ctx))
    def _():
        attend(k_ref.shape[1])


def _attention(q, k, v, lam, n_ctx):
    bsz, t, _ = q.shape
    n_ctx_tiles = n_ctx // ROW_TILE
    return pl.pallas_call(
        functools.partial(_attn_kernel, n_ctx=n_ctx, n_ctx_tiles=n_ctx_tiles),
        grid=(bsz, DA_HEADS, t // ROW_TILE),
        in_specs=[pl.BlockSpec((1, ROW_TILE, HEAD_W), lambda b, h, i: (b, i, h)),
                  pl.BlockSpec((1, t, HEAD_W), lambda b, h, i: (b, 0, h)),
                  pl.BlockSpec((1, t, HEAD_W), lambda b, h, i: (b, 0, h)),
                  pl.BlockSpec((8, HEAD_W), lambda b, h, i: (0, 0))],
        out_specs=pl.BlockSpec((1, ROW_TILE, HEAD_W), lambda b, h, i: (b, i, h)),
        out_shape=jax.ShapeDtypeStruct((bsz, t, BRANCH_W), F32),
        compiler_params=_cparams(("parallel", "parallel", "parallel")),
        name="diff_attn",
    )(q, k, v, lam)


def _dft_tables(n):
    idx = np.arange(n, dtype=np.int64)
    ang = 2.0 * np.pi * ((idx[:, None] * idx[None, :]) % n).astype(np.float64) / n
    scale = 1.0 / math.sqrt(n * FR_DG)
    return np.concatenate([np.cos(ang), -np.sin(ang)], axis=1).astype(np.float32) * np.float32(scale)


def _channel_dft_tables():
    idx = np.arange(FR_DG, dtype=np.int64)
    ang = 2.0 * np.pi * ((idx[:, None] * idx[None, :]) % FR_DG).astype(np.float64) / FR_DG
    eye = np.eye(FR_GROUPS)
    return (np.kron(eye, np.cos(ang)).astype(np.float32), np.kron(eye, np.sin(ang)).astype(np.float32))


def _fourier_kernel(u_ref, cc_ref, sc_ref, wc_ref, wl_ref, o_ref, zc_ref, zl_ref, *, n_ctx, n_ctx_tiles):
    i = pl.program_id(1)
    n_lat = u_ref.shape[1] - n_ctx

    @pl.when(i == 0)
    def _():
        u = u_ref[0].astype(BF16)
        zcos = _dot(u, cc_ref[...]).astype(BF16)
        zsin = _dot(u, sc_ref[...]).astype(BF16)
        zc_ref[0:n_ctx] = zcos[0:n_ctx]
        zc_ref[n_ctx:2 * n_ctx] = zsin[0:n_ctx]
        zl_ref[0:n_lat] = zcos[n_ctx:]
        zl_ref[n_lat:2 * n_lat] = zsin[n_ctx:]

    @pl.when(i < n_ctx_tiles)
    def _():
        o_ref[0] = _dot(wc_ref[...], zc_ref[...])

    @pl.when(i >= n_ctx_tiles)
    def _():
        o_ref[0] = _dot(wl_ref[...], zl_ref[...])


def _fourier(p, n_ctx):
    bsz, t, _ = p.shape
    n_lat = t - n_ctx
    n_ctx_tiles = n_ctx // ROW_TILE
    cc, sc = _channel_dft_tables()
    wc = jnp.asarray(_dft_tables(n_ctx)).astype(BF16)
    wl = jnp.asarray(_dft_tables(n_lat)).astype(BF16)
    cc = jnp.asarray(cc).astype(BF16)
    sc = jnp.asarray(sc).astype(BF16)
    return pl.pallas_call(
        functools.partial(_fourier_kernel, n_ctx=n_ctx, n_ctx_tiles=n_ctx_tiles),
        grid=(bsz, t // ROW_TILE),
        in_specs=[pl.BlockSpec((1, t, BRANCH_W), lambda b, i: (b, 0, COL_U)),
                  pl.BlockSpec((BRANCH_W, BRANCH_W), lambda b, i: (0, 0)),
                  pl.BlockSpec((BRANCH_W, BRANCH_W), lambda b, i: (0, 0)),
                  pl.BlockSpec((ROW_TILE, 2 * n_ctx), lambda b, i: (jnp.minimum(i, n_ctx_tiles - 1), 0)),
                  pl.BlockSpec((ROW_TILE, 2 * n_lat), lambda b, i: (jnp.maximum(i - n_ctx_tiles, 0), 0))],
        out_specs=pl.BlockSpec((1, ROW_TILE, BRANCH_W), lambda b, i: (b, i, 0)),
        out_shape=jax.ShapeDtypeStruct((bsz, t, BRANCH_W), F32),
        scratch_shapes=[pltpu.VMEM((2 * n_ctx, BRANCH_W), BF16), pltpu.VMEM((2 * n_lat, BRANCH_W), BF16)],
        compiler_params=_cparams(("parallel", "arbitrary")),
        name="fourier",
    )(p, cc, sc, wc, wl)


def _head_rms(x, gain):
    parts = []
    for h in range(BRANCH_W // HEAD_W):
        xh = x[:, h * HEAD_W:(h + 1) * HEAD_W]
        parts.append(xh * lax.rsqrt(jnp.mean(xh * xh, axis=-1, keepdims=True) + LN_EPS) * gain)
    return jnp.concatenate(parts, axis=-1)


def _merge_kernel(x_ref, mod_ref, gates_ref, og_ref, of_ref, ob_ref, at_ref, fr_ref,
                  hg_ref, da_ref, wa_ref, wb_ref, wc_ref, wo_ref, ln_ref, wr_ref,
                  x1_ref, h2_ref, lg_ref, *, alpha, attn_scale):
    d = x_ref.shape[2]
    a = _head_rms(of_ref[0] + ob_ref[0], hg_ref[...]) * _silu(og_ref[0])
    bq = _head_rms(at_ref[0], da_ref[...]) * attn_scale
    g = jax.nn.sigmoid(gates_ref[0])
    m = (g[:, 0:d] * _dot(a.astype(BF16), wa_ref[...])
         + g[:, d:2 * d] * _dot(bq.astype(BF16), wb_ref[...])
         + g[:, 2 * d:3 * d] * _dot(fr_ref[0].astype(BF16), wc_ref[...]))
    mix = _dot(m.astype(BF16), wo_ref[...])
    gate1 = mod_ref[0, 0, 2:3, :]
    shift2 = mod_ref[0, 0, 3:4, :]
    scale2 = mod_ref[0, 0, 4:5, :]
    x1 = _ln(alpha * x_ref[0] + gate1 * mix) * ln_ref[0:1, :] + ln_ref[1:2, :]
    x1_ref[0] = x1
    h2 = _ln(x1) * (1.0 + scale2) + shift2
    h2_ref[0] = h2.astype(BF16)
    lg_ref[...] = lax.dot_general(wr_ref[...], h2, (((1,), (1,)), ((), ())),
                                  preferred_element_type=F32, precision=lax.Precision.HIGHEST)


def _merge(x, mods, p, o_f, o_b, at, fr, hg_g, da_g, wa, wb, wc, wo, ln1, wr_t, alpha, attn_scale, n_ctx_tiles):
    bsz, t, d = x.shape
    tiles = t // ROW_TILE
    row = lambda w, col: pl.BlockSpec((1, ROW_TILE, w), lambda b, i: (b, i, col))
    full = lambda arr: pl.BlockSpec(arr.shape, lambda b, i: (0,) * arr.ndim)
    return pl.pallas_call(
        functools.partial(_merge_kernel, alpha=alpha, attn_scale=attn_scale),
        grid=(bsz, tiles),
        in_specs=[row(d, 0),
                  pl.BlockSpec((1, 1, 6, d), lambda b, i: (b, (i >= n_ctx_tiles).astype(jnp.int32), 0, 0)),
                  row(GATES_BLOCKS * BRANCH_W, 0), row(BRANCH_W, COL_OG),
                  row(BRANCH_W, 0), row(BRANCH_W, 0), row(BRANCH_W, 0), row(BRANCH_W, 0),
                  full(hg_g), full(da_g), full(wa), full(wb), full(wc), full(wo), full(ln1), full(wr_t)],
        out_specs=[row(d, 0), row(d, 0),
                   pl.BlockSpec((N_EXPERTS, ROW_TILE), lambda b, i: (0, b * tiles + i))],
        out_shape=[jax.ShapeDtypeStruct((bsz, t, d), F32),
                   jax.ShapeDtypeStruct((bsz, t, d), BF16),
                   jax.ShapeDtypeStruct((N_EXPERTS, bsz * t), F32)],
        compiler_params=_cparams(("parallel", "parallel")),
        name="merge",
    )(x, mods, p, p, o_f, o_b, at, fr, hg_g, da_g, wa, wb, wc, wo, ln1, wr_t)


def _router_kernel(lg_ref, bias_ref, w_ref):
    per = N_EXPERTS // N_GROUPS
    scores = [jax.nn.sigmoid(lg_ref[j * N_GROUPS:(j + 1) * N_GROUPS, :]) for j in range(per)]
    sel = [scores[j] + bias_ref[j * N_GROUPS:(j + 1) * N_GROUPS, :] for j in range(per)]
    shape = sel[0].shape
    grp = lax.broadcasted_iota(jnp.int32, shape, 0)

    def beats(other, me, tie):
        if isinstance(tie, bool):
            return jnp.where((other >= me) if tie else (other > me), 1, 0)
        return jnp.where(other > me, 1, jnp.where(other == me, tie, 0))

    m1 = jnp.maximum(sel[0], sel[1])
    m2 = jnp.minimum(sel[0], sel[1])
    for j in range(2, per):
        m2 = jnp.maximum(m2, jnp.minimum(m1, sel[j]))
        m1 = jnp.maximum(m1, sel[j])
    gscore = m1 + m2
    grank = jnp.zeros(shape, jnp.int32)
    lower_group = [None] + [jnp.where(grp >= kk, 1, 0) for kk in range(1, N_GROUPS)]
    for kk in range(1, N_GROUPS):
        other = pltpu.roll(gscore, kk, 0)
        grank = grank + beats(other, gscore, lower_group[kk])
    gmask = grank < TOPK_GROUPS
    sel = [jnp.where(gmask, s, MASK_VALUE) for s in sel]
    rolled = [[None] + [pltpu.roll(s, kk, 0) for kk in range(1, N_GROUPS)] for s in sel]
    weights = []
    for j in range(per):
        rank = jnp.zeros(shape, jnp.int32)
        for j2 in range(per):
            for kk in range(N_GROUPS):
                if kk == 0 and j2 == j:
                    continue
                if kk == 0:
                    rank = rank + beats(sel[j2], sel[j], j2 < j)
                else:
                    rank = rank + beats(rolled[j2][kk], sel[j], lower_group[kk])
        weights.append(jnp.where(rank < TOP_K, scores[j], 0.0))
    total = weights[0]
    for j in range(1, per):
        total = total + weights[j]
    total = jnp.sum(total, axis=0, keepdims=True)
    for j in range(per):
        w_ref[j * N_GROUPS:(j + 1) * N_GROUPS, :] = weights[j] / total * ROUTED_SCALE


def _router(logits_t, bias_col):
    e, n = logits_t.shape
    return pl.pallas_call(
        _router_kernel,
        grid=(n // ROUTER_TILE,),
        in_specs=[pl.BlockSpec((e, ROUTER_TILE), lambda i: (0, i)),
                  pl.BlockSpec((e, ROUTER_TILE), lambda i: (0, 0))],
        out_specs=pl.BlockSpec((e, ROUTER_TILE), lambda i: (0, i)),
        out_shape=jax.ShapeDtypeStruct((e, n), F32),
        compiler_params=_cparams(("parallel",)),
        name="router",
    )(logits_t, bias_col)


SEG_ALIGN = 8
TILE_ROWS = ROW_TILE * TOP_K + N_EXPERTS * SEG_ALIGN
CHUNKS_PER_BLOCK = 32
EXPERT_BLOCK = CHUNKS_PER_BLOCK * SEG_ALIGN
PACK_W = 512
ROW_W = PACK_W + HEAD_W
EXPERT_LANES = 128
HI_MASK = -65536


def _pack_pairs(x):
    hi = pltpu.bitcast(x[:, :PACK_W].astype(jnp.bfloat16).astype(F32), jnp.int32)
    lo = pltpu.bitcast(x[:, PACK_W:].astype(jnp.bfloat16).astype(F32), jnp.int32)
    return hi | lax.shift_right_logical(lo, jnp.full_like(lo, 16))


def _unpack_pairs(u):
    hi = pltpu.bitcast(u & HI_MASK, F32)
    lo = pltpu.bitcast(lax.shift_left(u, jnp.full_like(u, 16)), F32)
    return jnp.concatenate([hi, lo], axis=1).astype(BF16)


def _ceil_seg(c):
    return jnp.floor((c + (SEG_ALIGN - 1.0)) * (1.0 / SEG_ALIGN)) * SEG_ALIGN


def _split_256(c):
    hi = jnp.floor(c * (1.0 / 256.0))
    return hi.astype(BF16), (c - 256.0 * hi).astype(BF16)


def _moe_tables(tm):
    t = np.arange(tm)
    e = np.arange(EXPERT_LANES)
    incl_upper = (t[:, None] <= t[None, :]).astype(np.float32)
    strict_upper = (e[:, None] < e[None, :]).astype(np.float32)
    return incl_upper, strict_upper


def _dispatch_kernel(h_ref, wt_ref, wtok_ref, iu_ref, su_ref, o_ref):
    tm = h_ref.shape[0]
    r1 = o_ref.shape[0]
    sel = jnp.where(wt_ref[...] > 0.0, 1.0, 0.0)
    sel = jnp.concatenate([sel, jnp.zeros((EXPERT_LANES - N_EXPERTS, tm), F32)], axis=0)
    chosen = sel > 0.5
    mask = sel.astype(BF16)
    cnt = _dot(mask, iu_ref[...])
    ones_lanes = jnp.ones((tm, EXPERT_LANES), BF16)
    len_col = _ceil_seg(_dot(mask, ones_lanes))
    off_col = _dot_tn(su_ref[...], len_col.astype(BF16))
    len_row = _ceil_seg(_dot_nt(jnp.ones((8, tm), BF16), mask))
    off_row = _dot(len_row.astype(BF16), su_ref[...])
    dest = jnp.where(chosen, cnt + jnp.concatenate([off_col] * (tm // EXPERT_LANES), axis=1) - 1.0, -1.0)
    dhi, dlo = _split_256(dest)
    rows_e = lax.broadcasted_iota(jnp.int32, (r1, EXPERT_LANES), 0).astype(F32)
    lo = off_row[0:1, :]
    seg = jnp.where(rows_e >= lo, jnp.where(rows_e < lo + len_row[0:1, :], 1.0, 0.0), 0.0)
    segb = seg.astype(BF16)
    want = 256.0 * _dot(segb, dhi) + _dot(segb, dlo)
    rows_t = lax.broadcasted_iota(jnp.int32, (r1, tm), 0).astype(F32)
    onehot = jnp.where(want == rows_t, 1.0, 0.0).astype(BF16)
    xs = _dot(onehot, h_ref[...])
    wtok = wtok_ref[...]
    w_hi = wtok.astype(BF16)
    w_lo = (wtok - w_hi.astype(F32)).astype(BF16)
    w_row = jnp.sum((_dot(onehot, w_hi) + _dot(onehot, w_lo)) * seg, axis=1, keepdims=True)
    o_ref[:, 0:PACK_W] = _pack_pairs(xs)
    o_ref[:, PACK_W:ROW_W] = pltpu.bitcast(jnp.broadcast_to(w_row, (r1, HEAD_W)), jnp.int32)


def _dispatch(h2, w_t, w_tok):
    n, d = h2.shape
    tiles = n // ROW_TILE
    iu, su = _moe_tables(ROW_TILE)
    return pl.pallas_call(
        _dispatch_kernel,
        grid=(tiles,),
        in_specs=[pl.BlockSpec((ROW_TILE, d), lambda i: (i, 0)),
                  pl.BlockSpec((N_EXPERTS, ROW_TILE), lambda i: (0, i)),
                  pl.BlockSpec((ROW_TILE, EXPERT_LANES), lambda i: (i, 0)),
                  pl.BlockSpec((ROW_TILE, ROW_TILE), lambda i: (0, 0)),
                  pl.BlockSpec((EXPERT_LANES, EXPERT_LANES), lambda i: (0, 0))],
        out_specs=pl.BlockSpec((TILE_ROWS, ROW_W), lambda i: (i, 0)),
        out_shape=jax.ShapeDtypeStruct((tiles * TILE_ROWS, ROW_W), jnp.int32),
        compiler_params=_cparams(("parallel",)),
        name="moe_dispatch",
    )(h2, w_t, w_tok, jnp.asarray(iu, BF16), jnp.asarray(su, BF16))


def _moe_plan(w_t, tiles):
    cb = CHUNKS_PER_BLOCK
    cnt = jnp.sum((w_t > 0.0).reshape(N_EXPERTS, tiles, ROW_TILE), axis=-1, dtype=jnp.int32)
    nch = (cnt + SEG_ALIGN - 1) // SEG_ALIGN
    tile_chunk0 = jnp.cumsum(nch, axis=0) - nch
    per_expert = jnp.sum(nch, axis=1)
    padded = (per_expert + cb - 1) // cb * cb
    expert_end = jnp.cumsum(padded)
    expert_start = expert_end - padded
    seg_start = (expert_start[:, None] + jnp.cumsum(nch, axis=1) - nch).reshape(-1)
    seg_len = nch.reshape(-1)
    seg_src = (jnp.arange(tiles, dtype=jnp.int32)[None, :] * (TILE_ROWS // SEG_ALIGN) + tile_chunk0).reshape(-1)
    n_tok = tiles * ROW_TILE
    max_chunks = n_tok * TOP_K // SEG_ALIGN + tiles * N_EXPERTS + N_EXPERTS * (cb - 1)
    n_blocks = -(-max_chunks // cb)
    slot = jnp.arange(n_blocks * cb, dtype=jnp.int32)
    s = jnp.searchsorted(seg_start, slot, side='right').astype(jnp.int32) - 1
    c = slot - seg_start[s]
    src = jnp.where(c < seg_len[s], seg_src[s] + c, -1).astype(jnp.int32)
    blk = jnp.arange(n_blocks, dtype=jnp.int32) * cb
    blk_e = jnp.minimum(jnp.searchsorted(expert_end, blk, side='right'), N_EXPERTS - 1).astype(jnp.int32)
    n_active = (expert_end[-1] // cb).astype(jnp.int32).reshape(1)
    return blk_e, n_active, src


def _expert_kernel(blk_e_ref, nact_ref, src_ref, xs_hbm, wg_ref, wu_ref, wd_ref, out_hbm,
                   xbuf, obuf, gsem, ssem):
    del blk_e_ref
    b = pl.program_id(0)
    n_active = nact_ref[0]
    cb = CHUNKS_PER_BLOCK

    def chunk_copies(blk, slot, gather, start):
        for k in range(cb):
            ch = src_ref[blk * cb + k]

            @pl.when(ch >= 0)
            def _():
                rows = pl.ds(pl.multiple_of(ch * SEG_ALIGN, SEG_ALIGN), SEG_ALIGN)
                local = pl.ds(k * SEG_ALIGN, SEG_ALIGN)
                if gather:
                    cp = pltpu.make_async_copy(xs_hbm.at[rows], xbuf.at[slot, local], gsem.at[slot])
                else:
                    cp = pltpu.make_async_copy(obuf.at[slot, local], out_hbm.at[rows], ssem.at[slot])
                if start:
                    cp.start()
                else:
                    cp.wait()

    @pl.when(b == 0)
    def _():
        xbuf[...] = jnp.zeros_like(xbuf)
        chunk_copies(0, 0, True, True)

    @pl.when(b < n_active)
    def _():
        slot = b % 2

        @pl.when(b + 1 < n_active)
        def _():
            chunk_copies(b + 1, 1 - slot, True, True)

        chunk_copies(b, slot, True, False)

        @pl.when(b >= 2)
        def _():
            chunk_copies(b - 2, slot, False, False)

        x = xbuf[slot]
        xb = _unpack_pairs(x[:, 0:PACK_W])
        w = pltpu.bitcast(x[:, PACK_W:ROW_W], F32)
        hid = _silu(_dot(xb, wg_ref[0])) * _dot(xb, wu_ref[0]) * jnp.concatenate([w, w], axis=1)
        obuf[slot, :, 0:PACK_W] = _pack_pairs(_dot(hid.astype(BF16), wd_ref[0]))
        obuf[slot, :, PACK_W:ROW_W] = x[:, PACK_W:ROW_W]
        chunk_copies(b, slot, False, True)

        @pl.when(b == n_active - 1)
        def _():
            chunk_copies(b, slot, False, False)

            @pl.when(b >= 1)
            def _():
                chunk_copies(b - 1, 1 - slot, False, False)


def _experts(xs, plan, wg, wu, wd):
    blk_e, n_active, src = plan
    n_blocks = blk_e.shape[0]
    d = wg.shape[1]
    wspec = lambda shape: pl.BlockSpec((1,) + shape, lambda b, be, na, sr: (be[b], 0, 0))
    return pl.pallas_call(
        _expert_kernel,
        grid_spec=pltpu.PrefetchScalarGridSpec(
            num_scalar_prefetch=3,
            grid=(n_blocks,),
            in_specs=[pl.BlockSpec(memory_space=pl.ANY),
                      wspec((d, EXPERT_DIM)), wspec((d, EXPERT_DIM)), wspec((EXPERT_DIM, d))],
            out_specs=pl.BlockSpec(memory_space=pl.ANY),
            scratch_shapes=[pltpu.VMEM((2, EXPERT_BLOCK, ROW_W), jnp.int32),
                            pltpu.VMEM((2, EXPERT_BLOCK, ROW_W), jnp.int32),
                            pltpu.SemaphoreType.DMA((2,)), pltpu.SemaphoreType.DMA((2,))]),
        out_shape=jax.ShapeDtypeStruct(xs.shape, xs.dtype),
        input_output_aliases={3: 0},
        compiler_params=_cparams(("arbitrary",)),
        name="moe_experts",
    )(blk_e, n_active, src, xs, wg, wu, wd)


def _combine_kernel(ys_ref, wtok_ref, h_ref, x1_ref, mod_ref, ln_ref, il_ref, su_ref,
                    sg_ref, su2_ref, sd_ref, o_ref, *, alpha):
    tm = h_ref.shape[0]
    r1 = ys_ref.shape[0]
    chosen = wtok_ref[...] > 0.0
    mask = jnp.where(chosen, 1.0, 0.0).astype(BF16)
    cnt = _dot(il_ref[...], mask)
    len_row = _ceil_seg(_dot(jnp.ones((8, tm), BF16), mask))
    off_row = _dot(len_row.astype(BF16), su_ref[...])
    dhi, dlo = _split_256(jnp.where(chosen, cnt + off_row[0:1, :] - 1.0, -1.0))
    len_col = _ceil_seg(_dot_tn(mask, jnp.ones((tm, EXPERT_LANES), BF16)))
    off_col = _dot_tn(su_ref[...], len_col.astype(BF16))
    reps = r1 // EXPERT_LANES
    lo = jnp.concatenate([off_col] * reps, axis=1)
    hi = lo + jnp.concatenate([len_col] * reps, axis=1)
    rows_e = lax.broadcasted_iota(jnp.int32, (EXPERT_LANES, r1), 1).astype(F32)
    segb = jnp.where(rows_e >= lo, jnp.where(rows_e < hi, 1.0, 0.0), 0.0).astype(BF16)
    want = 256.0 * _dot(dhi, segb) + _dot(dlo, segb)
    rows_t = lax.broadcasted_iota(jnp.int32, (tm, r1), 1).astype(F32)
    onehot = jnp.where(want == rows_t, 1.0, 0.0).astype(BF16)
    y = _dot(onehot, _unpack_pairs(ys_ref[:, 0:PACK_W]))
    x = h_ref[...]
    y = y + _dot((_silu(_dot(x, sg_ref[...])) * _dot(x, su2_ref[...])).astype(BF16), sd_ref[...])
    gate2 = mod_ref[0, 0, 5:6, :]
    o_ref[...] = _ln(alpha * x1_ref[...] + gate2 * y) * ln_ref[0:1, :] + ln_ref[1:2, :]


def _combine(ys, w_tok, h2, x1, mods, ln2, sg, su, sd, alpha, tiles_per_seq, n_ctx_tiles):
    n, d = h2.shape
    tiles = n // ROW_TILE
    iu, sup = _moe_tables(ROW_TILE)
    full = lambda arr: pl.BlockSpec(arr.shape, lambda i: (0,) * arr.ndim)
    row = pl.BlockSpec((ROW_TILE, d), lambda i: (i, 0))
    il = jnp.asarray(iu.T, BF16)
    sup = jnp.asarray(sup, BF16)
    return pl.pallas_call(
        functools.partial(_combine_kernel, alpha=alpha),
        grid=(tiles,),
        in_specs=[pl.BlockSpec((TILE_ROWS, ROW_W), lambda i: (i, 0)),
                  pl.BlockSpec((ROW_TILE, EXPERT_LANES), lambda i: (i, 0)),
                  row, row,
                  pl.BlockSpec((1, 1, 6, d), lambda i: (i // tiles_per_seq,
                                                        (i % tiles_per_seq >= n_ctx_tiles).astype(jnp.int32), 0, 0)),
                  full(ln2), full(il), full(sup), full(sg), full(su), full(sd)],
        out_specs=row,
        out_shape=jax.ShapeDtypeStruct((n, d), F32),
        compiler_params=_cparams(("parallel",)),
        name="moe_combine",
    )(ys, w_tok, h2, x1, mods, ln2, il, sup, sg, su, sd)


def _rope_tables(n_lat, n_ctx):
    rows = n_lat // GRID_W
    row = jnp.repeat(jnp.arange(rows, dtype=F32), GRID_W)
    col = (jnp.arange(n_lat) % GRID_W).astype(F32)
    n_freq = DA_DH // 4
    inv = ROPE_THETA ** (-jnp.arange(n_freq, dtype=F32) / n_freq)
    ang = jnp.concatenate([row[:, None] * inv, col[:, None] * inv], axis=-1)
    ang = jnp.concatenate([ang, ang], axis=-1)
    cos, sin = jnp.cos(ang), jnp.sin(ang)
    sign = jnp.where(jnp.arange(DA_DH) < DA_DH // 2, -1.0, 1.0).astype(F32)
    cos = jnp.tile(cos, (1, HEAD_W // DA_DH))
    sin = jnp.tile(sin * sign, (1, HEAD_W // DA_DH))
    cos = jnp.concatenate([jnp.ones((n_ctx, HEAD_W), F32), cos], axis=0)
    sin = jnp.concatenate([jnp.zeros((n_ctx, HEAD_W), F32), sin], axis=0)
    return cos, sin


def kernel(x, c, ctx, c_ctx, w_ada, b_ada, w_in, lb_logits, hg_norm_g, lam_params, da_norm_g, w_branch_a, w_branch_b, w_branch_c, w_out, ln1_g, ln1_b, ln2_g, ln2_b, w_router, router_bias, w_exp_gate, w_exp_up, w_exp_down, w_sh_gate, w_sh_up, w_sh_down):
    bsz, n_lat, d = x.shape
    n_ctx = ctx.shape[1]
    depth = w_in.shape[0]
    t = n_ctx + n_lat
    assert n_ctx % ROW_TILE == 0 and n_lat % ROW_TILE == 0
    assert n_ctx % HG_CHUNK == 0 and n_lat % HG_CHUNK == 0 and n_lat % GRID_W == 0
    n_ctx_tiles = n_ctx // ROW_TILE
    alpha = (2 * depth) ** 0.25
    per = N_EXPERTS // N_GROUPS

    cos, sin = _rope_tables(n_lat, n_ctx)
    sm = jax.nn.softmax(lb_logits.astype(F32), axis=0)
    lower = jnp.cumsum(sm, axis=0) - sm[0:1]
    log_lb = jnp.log(jnp.maximum(lower, LB_FLOOR))
    log1m_lb = jnp.log1p(-lower)

    n_rows = -(-(bsz + 1) // 8) * 8
    cvec = jnp.concatenate([c, c_ctx[None, :], jnp.zeros((n_rows - bsz - 1, d), F32)], axis=0)

    n_branch_cols = w_in.shape[2] - 3 * d

    def member_major(a):
        return a.reshape((N_GROUPS, per) + a.shape[1:]).swapaxes(0, 1).reshape(a.shape)

    xs = jnp.concatenate([ctx, x], axis=1)
    for l in range(depth):
        mod = _ada(cvec, w_ada[l], b_ada[l])
        mod_lat = mod[:bsz].reshape(bsz, 1, 6, d)
        mod_ctx = jnp.broadcast_to(mod[bsz].reshape(1, 1, 6, d), (bsz, 1, 6, d))
        mods = jnp.concatenate([mod_ctx, mod_lat], axis=1)

        w_in_l = jnp.concatenate([w_in[l][:, n_branch_cols:], w_in[l][:, :n_branch_cols]], axis=1)
        p = _inproj(xs, mods, w_in_l.astype(BF16), n_ctx_tiles)
        lb_f = jnp.stack([log_lb[l, 0], log1m_lb[l, 0]])
        lb_b = jnp.stack([log_lb[l, 1], log1m_lb[l, 1]])
        o_f = _hgrn(p, lb_f, COL_FF, False, n_ctx)
        o_b = _hgrn(p, lb_b, COL_FB, True, n_ctx)

        lam_init = 0.8 - 0.6 * math.exp(-0.3 * l)
        lp = lam_params[l].astype(F32)
        lam = jnp.exp(jnp.sum(lp[0] * lp[1])) - jnp.exp(jnp.sum(lp[2] * lp[3])) + lam_init
        qr, kr, vr = _attn_prep(p, cos, sin)
        at = _attention(qr, kr, vr, jnp.full((8, HEAD_W), lam, F32), n_ctx)
        fr = _fourier(p, n_ctx)

        ln1 = jnp.stack([ln1_g[l], ln1_b[l]])
        hg_g = hg_norm_g[l].reshape(1, HEAD_W)
        da_g = da_norm_g[l].reshape(1, HEAD_W)
        x1, h2, logits_t = _merge(
            xs, mods, p, o_f, o_b, at, fr, hg_g, da_g,
            w_branch_a[l].astype(BF16), w_branch_b[l].astype(BF16), w_branch_c[l].astype(BF16),
            w_out[l].astype(BF16), ln1, member_major(w_router[l].T), alpha, 1.0 - lam_init, n_ctx_tiles)

        bias_col = jnp.broadcast_to(member_major(router_bias[l])[:, None], (N_EXPERTS, ROUTER_TILE))
        wts_t = _router(logits_t, bias_col)
        n_tok = bsz * t
        w_tok = jnp.concatenate([wts_t.T, jnp.zeros((n_tok, EXPERT_LANES - N_EXPERTS), F32)], axis=1)
        h2f = h2.reshape(n_tok, d)
        rows = _dispatch(h2f, wts_t, w_tok)
        plan = _moe_plan(wts_t, n_tok // ROW_TILE)
        rows = _experts(rows, plan, member_major(w_exp_gate[l]).astype(BF16),
                        member_major(w_exp_up[l]).astype(BF16), member_major(w_exp_down[l]).astype(BF16))
        ln2 = jnp.stack([ln2_g[l], ln2_b[l]])
        xs = _combine(rows, w_tok, h2f, x1.reshape(n_tok, d), mods, ln2,
                      w_sh_gate[l].astype(BF16), w_sh_up[l].astype(BF16), w_sh_down[l].astype(BF16),
                      alpha, t // ROW_TILE, n_ctx_tiles).reshape(bsz, t, d)
    return xs[:, n_ctx:]
```

```python
import functools
import math

import numpy as np
import jax
import jax.numpy as jnp
from jax import lax
from jax.experimental import pallas as pl
from jax.experimental.pallas import tpu as pltpu

F32 = jnp.float32
BF16 = jnp.bfloat16

HG_HEADS = 4
HG_DK = 128
LB_FLOOR = 1e-30
DA_HEADS = 4
DA_DH = 64
ROPE_THETA = 10000.0
GRID_W = 64
FR_GROUPS = 4
FR_DG = 128
N_EXPERTS = 64
TOP_K = 8
EXPERT_DIM = 256
N_GROUPS = 8
TOPK_GROUPS = 4
ROUTED_SCALE = 2.5
MASK_VALUE = -1e9
LN_EPS = 1e-5

HEAD_W = 128
BRANCH_W = 512
GATES_BLOCKS = 6
COL_Q, COL_FF, COL_FB, COL_I, COL_OG, COL_QA, COL_KA, COL_VA, COL_U = range(6, 15)

ROW_TILE = 256
HG_CHUNK = 128
HG_LEVELS = (64, 32, 16, 8, 4, 2, 1)
IN_N_TILE = 2560
IN_ROW_TILE = 768
ROUTER_TILE = 512
VMEM_LIMIT = 48 * 1024 * 1024


def _cparams(sem):
    return pltpu.CompilerParams(dimension_semantics=sem, vmem_limit_bytes=VMEM_LIMIT)


def _dot(a, b):
    return jnp.dot(a, b, preferred_element_type=F32)


def _dot_nt(a, b):
    return lax.dot_general(a, b, (((1,), (1,)), ((), ())), preferred_element_type=F32)


def _dot_tn(a, b):
    return lax.dot_general(a, b, (((0,), (0,)), ((), ())), preferred_element_type=F32)


def _silu(x):
    return x * jax.nn.sigmoid(x)


def _ln(x):
    mu = jnp.mean(x, axis=-1, keepdims=True)
    xc = x - mu
    var = jnp.mean(xc * xc, axis=-1, keepdims=True)
    return xc * lax.rsqrt(var + LN_EPS)


def _ada_kernel(c_ref, w_ref, b_ref, o_ref):
    o_ref[...] = jnp.dot(_silu(c_ref[...]), w_ref[...], preferred_element_type=F32,
                         precision=lax.Precision.HIGHEST) + b_ref[...]


def _ada(cvec, w, b):
    rows, d = cvec.shape
    n = w.shape[1]
    tn = 1536
    return pl.pallas_call(
        _ada_kernel,
        grid=(n // tn,),
        in_specs=[pl.BlockSpec((rows, d), lambda j: (0, 0)),
                  pl.BlockSpec((d, tn), lambda j: (0, j)),
                  pl.BlockSpec((1, tn), lambda j: (0, j))],
        out_specs=pl.BlockSpec((rows, tn), lambda j: (0, j)),
        out_shape=jax.ShapeDtypeStruct((rows, n), F32),
        compiler_params=_cparams(("parallel",)),
        name="ada",
    )(cvec, w, b.reshape(1, n))


def _inproj_kernel(x_ref, mod_ref, w_ref, o_ref, *, n_ctx):
    first = pl.program_id(2) == 0
    xn = _ln(x_ref[0])
    shift_top = jnp.where(first, mod_ref[0, 0, 0:1, :], mod_ref[0, 1, 0:1, :])
    scale_top = jnp.where(first, mod_ref[0, 0, 1:2, :], mod_ref[0, 1, 1:2, :])
    parts = [xn[:n_ctx] * (1.0 + scale_top) + shift_top]
    if xn.shape[0] > n_ctx:
        parts.append(xn[n_ctx:] * (1.0 + mod_ref[0, 1, 1:2, :]) + mod_ref[0, 1, 0:1, :])
    h = jnp.concatenate(parts, axis=0) if len(parts) > 1 else parts[0]
    o_ref[0] = _dot(h.astype(BF16), w_ref[...])


def _inproj(x, mods, w, n_ctx):
    bsz, t, d = x.shape
    n = w.shape[1]
    tm = IN_ROW_TILE if t % IN_ROW_TILE == 0 and n_ctx <= IN_ROW_TILE else n_ctx
    return pl.pallas_call(
        functools.partial(_inproj_kernel, n_ctx=n_ctx),
        grid=(n // IN_N_TILE, bsz, t // tm),
        in_specs=[pl.BlockSpec((1, tm, d), lambda j, b, i: (b, i, 0)),
                  pl.BlockSpec((1, 2, 6, d), lambda j, b, i: (b, 0, 0, 0)),
                  pl.BlockSpec((d, IN_N_TILE), lambda j, b, i: (0, j))],
        out_specs=pl.BlockSpec((1, tm, IN_N_TILE), lambda j, b, i: (b, i, j)),
        out_shape=jax.ShapeDtypeStruct((bsz, t, n), F32),
        compiler_params=_cparams(("parallel", "parallel", "parallel")),
        name="inproj",
    )(x, mods, w)


def _hgrn_tables(reverse):
    c = HG_CHUNK
    t = np.arange(c)
    if reverse:
        cum = (t[None, :] >= t[:, None])
    else:
        cum = (t[None, :] <= t[:, None])
    mats = [cum]
    for m in HG_LEVELS:
        idx = (t // (2 * m)) * (2 * m) + m - (1 if reverse else 0)
        mats.append(cum[idx])
    stack = np.concatenate(mats, axis=0).astype(np.float32)
    x = t[:, None] ^ t[None, :]
    lvl = np.zeros((c, c), np.int32)
    for li, m in enumerate(HG_LEVELS):
        lvl = np.where((x >= m) & (x < 2 * m), li + 1, lvl)
    visible = (t[None, :] >= t[:, None]) if reverse else (t[None, :] <= t[:, None])
    lvl = np.where(visible, lvl, -1).astype(np.int32)
    return stack, lvl


def _hgrn_head(q_ref, f_ref, v_ref, lb_ref, stack, lvl, o_ref, st_ref, h, reverse):
    c = HG_CHUNK
    sl = slice(h * HEAD_W, (h + 1) * HEAD_W)
    q = _silu(q_ref[0, :, sl])
    fr = f_ref[0, :, sl]
    v = v_ref[0, :, sl].astype(BF16)
    log_lb = lb_ref[0:1, sl]
    log1m_lb = lb_ref[1:2, sl]
    c2 = log1m_lb + jnp.minimum(fr, 0.0) - jnp.log1p(jnp.exp(-jnp.abs(fr)))
    g = jnp.maximum(log_lb, c2) + jnp.log1p(jnp.exp(-jnp.abs(log_lb - c2)))
    k = 1.0 - jnp.exp(g)
    g1 = g.astype(BF16)
    r1 = g - g1.astype(F32)
    g2 = r1.astype(BF16)
    g3 = (r1 - g2.astype(F32)).astype(BF16)
    bb = _dot(stack, g1) + _dot(stack, g2) + _dot(stack, g3)
    b = bb[0:c]
    a = jnp.where(lvl == 0, _dot_nt(q.astype(BF16), k.astype(BF16)), 0.0)
    for li in range(len(HG_LEVELS)):
        e = jnp.exp(-jnp.abs(b - bb[(li + 1) * c:(li + 2) * c]))
        a = jnp.where(lvl == li + 1, _dot_nt((q * e).astype(BF16), (k * e).astype(BF16)), a)
    st = st_ref[h]
    o = _dot(a.astype(BF16), v) + _dot_nt((q * jnp.exp(b)).astype(BF16), st.astype(BF16))
    o_ref[0, :, sl] = o
    b_end = b[0:1] if reverse else b[c - 1:c]
    k_end = (k * jnp.exp(b_end - b)).astype(BF16)
    st_ref[h] = st * jnp.exp(b_end) + _dot_tn(v, k_end)


def _hgrn_kernel(qf_ref, ff_ref, vf_ref, qb_ref, fb_ref, vb_ref, lbf_ref, lbb_ref,
                 stf_ref, lvf_ref, stb_ref, lvb_ref, of_ref, ob_ref, sf_ref, sb_ref):
    @pl.when(pl.program_id(1) == 0)
    def _():
        sf_ref[...] = jnp.zeros_like(sf_ref)
        sb_ref[...] = jnp.zeros_like(sb_ref)

    for h in range(HG_HEADS):
        _hgrn_head(qf_ref, ff_ref, vf_ref, lbf_ref, stf_ref[...], lvf_ref[...], of_ref, sf_ref, h, False)
        _hgrn_head(qb_ref, fb_ref, vb_ref, lbb_ref, stb_ref[...], lvb_ref[...], ob_ref, sb_ref, h, True)


def _hgrn(p, lb_f, lb_b, n_ctx):
    bsz, t, _ = p.shape
    c = HG_CHUNK
    nc, nt = n_ctx // c, t // c
    stack_f, lvl_f = _hgrn_tables(False)
    stack_b, lvl_b = _hgrn_tables(True)

    def rev(j):
        return jnp.where(j < nc, nc - 1 - j, nc + nt - 1 - j)

    nst = stack_f.shape[0]
    blk = lambda row, col: pl.BlockSpec((1, c, BRANCH_W), lambda b, j: (b, row(j), col))
    const = lambda shape: pl.BlockSpec(shape, lambda b, j: (0, 0))
    fwd = lambda j: j
    out = jax.ShapeDtypeStruct((bsz, t, BRANCH_W), F32)
    state = pltpu.VMEM((HG_HEADS, HEAD_W, HEAD_W), F32)
    return pl.pallas_call(
        _hgrn_kernel,
        grid=(bsz, nt),
        in_specs=[blk(fwd, COL_Q), blk(fwd, COL_FF), blk(fwd, COL_I),
                  blk(rev, COL_Q), blk(rev, COL_FB), blk(rev, COL_I),
                  const((2, BRANCH_W)), const((2, BRANCH_W)),
                  const((nst, c)), const((c, c)), const((nst, c)), const((c, c))],
        out_specs=[blk(fwd, 0), blk(rev, 0)],
        out_shape=[out, out],
        scratch_shapes=[state, state],
        compiler_params=_cparams(("parallel", "arbitrary")),
        name="hgrn",
    )(p, p, p, p, p, p, lb_f, lb_b, jnp.asarray(stack_f, BF16), jnp.asarray(lvl_f),
      jnp.asarray(stack_b, BF16), jnp.asarray(lvl_b))


def _attn_prep_kernel(q_ref, k_ref, v_ref, cos_ref, sin_ref, qo_ref, ko_ref, vo_ref):
    cos = cos_ref[...]
    sin = sin_ref[...]
    lane = lax.broadcasted_iota(jnp.int32, cos.shape, 1)
    first_half = (lane % DA_DH) < (DA_DH // 2)

    def rope(x):
        up = pltpu.roll(x, HEAD_W - DA_DH // 2, 1)
        down = pltpu.roll(x, DA_DH // 2, 1)
        return x * cos + jnp.where(first_half, up, down) * sin

    for h in range(DA_HEADS):
        sl = slice(h * HEAD_W, (h + 1) * HEAD_W)
        qo_ref[0, :, sl] = (rope(q_ref[0, :, sl]) * (DA_DH ** -0.5)).astype(BF16)
        ko_ref[0, :, sl] = rope(k_ref[0, :, sl]).astype(BF16)
    vo_ref[0] = v_ref[0].astype(BF16)


def _attn_prep(p, cos, sin):
    bsz, t, _ = p.shape
    spec = lambda col: pl.BlockSpec((1, ROW_TILE, BRANCH_W), lambda b, i: (b, i, col))
    tab = pl.BlockSpec((ROW_TILE, HEAD_W), lambda b, i: (i, 0))
    out = jax.ShapeDtypeStruct((bsz, t, BRANCH_W), BF16)
    return pl.pallas_call(
        _attn_prep_kernel,
        grid=(bsz, t // ROW_TILE),
        in_specs=[spec(COL_QA), spec(COL_KA), spec(COL_VA), tab, tab],
        out_specs=[spec(0), spec(0), spec(0)],
        out_shape=[out, out, out],
        compiler_params=_cparams(("parallel", "parallel")),
        name="attn_prep",
    )(p, p, p, cos, sin)


def _attn_kernel(q_ref, k_ref, v_ref, lam_ref, o_ref, *, n_ctx, n_ctx_tiles):
    lam = lam_ref[0:1, 0:1]

    def attend(n_keys):
        q = q_ref[0]
        k = k_ref[0, 0:n_keys, :]
        v = v_ref[0, 0:n_keys, :]
        lane = lax.broadcasted_iota(jnp.int32, q.shape, 1)
        zero = jnp.zeros_like(q)
        outs = []
        for first in (True, False):
            qm = jnp.where((lane < DA_DH) == first, q, zero)
            s = _dot_nt(qm, k)
            e = jnp.exp(s - jnp.max(s, axis=-1, keepdims=True))
            outs.append(_dot(e.astype(BF16), v) / jnp.sum(e, axis=-1, keepdims=True))
        o_ref[0] = outs[0] - lam * outs[1]

    is_ctx = pl.program_id(2) < n_ctx_tiles

    @pl.when(is_ctx)
    def _():
        attend(n_ctx)

    @pl.when(jnp.logical_not(is_ctx))
    def _():
        attend(k_ref.shape[1])


def _attention(q, k, v, lam, n_ctx):
    bsz, t, _ = q.shape
    n_ctx_tiles = n_ctx // ROW_TILE
    return pl.pallas_call(
        functools.partial(_attn_kernel, n_ctx=n_ctx, n_ctx_tiles=n_ctx_tiles),
        grid=(bsz, DA_HEADS, t // ROW_TILE),
        in_specs=[pl.BlockSpec((1, ROW_TILE, HEAD_W), lambda b, h, i: (b, i, h)),
                  pl.BlockSpec((1, t, HEAD_W), lambda b, h, i: (b, 0, h)),
                  pl.BlockSpec((1, t, HEAD_W), lambda b, h, i: (b, 0, h)),
                  pl.BlockSpec((8, HEAD_W), lambda b, h, i: (0, 0))],
        out_specs=pl.BlockSpec((1, ROW_TILE, HEAD_W), lambda b, h, i: (b, i, h)),
        out_shape=jax.ShapeDtypeStruct((bsz, t, BRANCH_W), F32),
        compiler_params=_cparams(("parallel", "parallel", "parallel")),
        name="diff_attn",
    )(q, k, v, lam)


def _dft_tables(n):
    idx = np.arange(n, dtype=np.int64)
    ang = 2.0 * np.pi * ((idx[:, None] * idx[None, :]) % n).astype(np.float64) / n
    scale = 1.0 / math.sqrt(n * FR_DG)
    return np.concatenate([np.cos(ang), -np.sin(ang)], axis=1).astype(np.float32) * np.float32(scale)


def _channel_dft_tables():
    idx = np.arange(FR_DG, dtype=np.int64)
    ang = 2.0 * np.pi * ((idx[:, None] * idx[None, :]) % FR_DG).astype(np.float64) / FR_DG
    eye = np.eye(FR_GROUPS)
    return (np.kron(eye, np.cos(ang)).astype(np.float32), np.kron(eye, np.sin(ang)).astype(np.float32))


def _fourier_kernel(u_ref, cc_ref, sc_ref, wc_ref, wl_ref, o_ref, zc_ref, zl_ref, *, n_ctx, n_ctx_tiles):
    i = pl.program_id(1)
    n_lat = u_ref.shape[1] - n_ctx

    @pl.when(i == 0)
    def _():
        u = u_ref[0].astype(BF16)
        zcos = _dot(u, cc_ref[...]).astype(BF16)
        zsin = _dot(u, sc_ref[...]).astype(BF16)
        zc_ref[0:n_ctx] = zcos[0:n_ctx]
        zc_ref[n_ctx:2 * n_ctx] = zsin[0:n_ctx]
        zl_ref[0:n_lat] = zcos[n_ctx:]
        zl_ref[n_lat:2 * n_lat] = zsin[n_ctx:]

    @pl.when(i < n_ctx_tiles)
    def _():
        o_ref[0] = _dot(wc_ref[...], zc_ref[...])

    @pl.when(i >= n_ctx_tiles)
    def _():
        o_ref[0] = _dot(wl_ref[...], zl_ref[...])


def _fourier(p, n_ctx):
    bsz, t, _ = p.shape
    n_lat = t - n_ctx
    n_ctx_tiles = n_ctx // ROW_TILE
    cc, sc = _channel_dft_tables()
    wc = jnp.asarray(_dft_tables(n_ctx)).astype(BF16)
    wl = jnp.asarray(_dft_tables(n_lat)).astype(BF16)
    cc = jnp.asarray(cc).astype(BF16)
    sc = jnp.asarray(sc).astype(BF16)
    return pl.pallas_call(
        functools.partial(_fourier_kernel, n_ctx=n_ctx, n_ctx_tiles=n_ctx_tiles),
        grid=(bsz, t // ROW_TILE),
        in_specs=[pl.BlockSpec((1, t, BRANCH_W), lambda b, i: (b, 0, COL_U)),
                  pl.BlockSpec((BRANCH_W, BRANCH_W), lambda b, i: (0, 0)),
                  pl.BlockSpec((BRANCH_W, BRANCH_W), lambda b, i: (0, 0)),
                  pl.BlockSpec((ROW_TILE, 2 * n_ctx), lambda b, i: (jnp.minimum(i, n_ctx_tiles - 1), 0)),
                  pl.BlockSpec((ROW_TILE, 2 * n_lat), lambda b, i: (jnp.maximum(i - n_ctx_tiles, 0), 0))],
        out_specs=pl.BlockSpec((1, ROW_TILE, BRANCH_W), lambda b, i: (b, i, 0)),
        out_shape=jax.ShapeDtypeStruct((bsz, t, BRANCH_W), F32),
        scratch_shapes=[pltpu.VMEM((2 * n_ctx, BRANCH_W), BF16), pltpu.VMEM((2 * n_lat, BRANCH_W), BF16)],
        compiler_params=_cparams(("parallel", "arbitrary")),
        name="fourier",
    )(p, cc, sc, wc, wl)


def _head_rms(x, gain):
    parts = []
    for h in range(BRANCH_W // HEAD_W):
        xh = x[:, h * HEAD_W:(h + 1) * HEAD_W]
        parts.append(xh * lax.rsqrt(jnp.mean(xh * xh, axis=-1, keepdims=True) + LN_EPS) * gain)
    return jnp.concatenate(parts, axis=-1)


def _merge_kernel(x_ref, mod_ref, gates_ref, og_ref, of_ref, ob_ref, at_ref, fr_ref,
                  hg_ref, da_ref, wa_ref, wb_ref, wc_ref, wo_ref, ln_ref, wr_ref,
                  x1_ref, h2_ref, lg_ref, *, alpha, attn_scale):
    d = x_ref.shape[2]
    a = _head_rms(of_ref[0] + ob_ref[0], hg_ref[...]) * _silu(og_ref[0])
    bq = _head_rms(at_ref[0], da_ref[...]) * attn_scale
    g = jax.nn.sigmoid(gates_ref[0])
    m = (g[:, 0:d] * _dot(a.astype(BF16), wa_ref[...])
         + g[:, d:2 * d] * _dot(bq.astype(BF16), wb_ref[...])
         + g[:, 2 * d:3 * d] * _dot(fr_ref[0].astype(BF16), wc_ref[...]))
    mix = _dot(m.astype(BF16), wo_ref[...])
    gate1 = mod_ref[0, 0, 2:3, :]
    shift2 = mod_ref[0, 0, 3:4, :]
    scale2 = mod_ref[0, 0, 4:5, :]
    x1 = _ln(alpha * x_ref[0] + gate1 * mix) * ln_ref[0:1, :] + ln_ref[1:2, :]
    x1_ref[0] = x1
    h2 = _ln(x1) * (1.0 + scale2) + shift2
    h2_ref[0] = h2.astype(BF16)
    lg_ref[...] = lax.dot_general(wr_ref[...], h2, (((1,), (1,)), ((), ())),
                                  preferred_element_type=F32, precision=lax.Precision.HIGHEST)


def _merge(x, mods, p, o_f, o_b, at, fr, hg_g, da_g, wa, wb, wc, wo, ln1, wr_t, alpha, attn_scale, n_ctx_tiles):
    bsz, t, d = x.shape
    tiles = t // ROW_TILE
    row = lambda w, col: pl.BlockSpec((1, ROW_TILE, w), lambda b, i: (b, i, col))
    full = lambda arr: pl.BlockSpec(arr.shape, lambda b, i: (0,) * arr.ndim)
    return pl.pallas_call(
        functools.partial(_merge_kernel, alpha=alpha, attn_scale=attn_scale),
        grid=(bsz, tiles),
        in_specs=[row(d, 0),
                  pl.BlockSpec((1, 1, 6, d), lambda b, i: (b, (i >= n_ctx_tiles).astype(jnp.int32), 0, 0)),
                  row(GATES_BLOCKS * BRANCH_W, 0), row(BRANCH_W, COL_OG),
                  row(BRANCH_W, 0), row(BRANCH_W, 0), row(BRANCH_W, 0), row(BRANCH_W, 0),
                  full(hg_g), full(da_g), full(wa), full(wb), full(wc), full(wo), full(ln1), full(wr_t)],
        out_specs=[row(d, 0), row(d, 0),
                   pl.BlockSpec((N_EXPERTS, ROW_TILE), lambda b, i: (0, b * tiles + i))],
        out_shape=[jax.ShapeDtypeStruct((bsz, t, d), F32),
                   jax.ShapeDtypeStruct((bsz, t, d), BF16),
                   jax.ShapeDtypeStruct((N_EXPERTS, bsz * t), F32)],
        compiler_params=_cparams(("parallel", "parallel")),
        name="merge",
    )(x, mods, p, p, o_f, o_b, at, fr, hg_g, da_g, wa, wb, wc, wo, ln1, wr_t)


def _router_kernel(lg_ref, bias_ref, w_ref):
    per = N_EXPERTS // N_GROUPS
    scores = [jax.nn.sigmoid(lg_ref[j * N_GROUPS:(j + 1) * N_GROUPS, :]) for j in range(per)]
    sel = [scores[j] + bias_ref[j * N_GROUPS:(j + 1) * N_GROUPS, :] for j in range(per)]
    shape = sel[0].shape
    grp = lax.broadcasted_iota(jnp.int32, shape, 0)

    def beats(other, me, tie):
        if isinstance(tie, bool):
            return jnp.where((other >= me) if tie else (other > me), 1, 0)
        return jnp.where(other > me, 1, jnp.where(other == me, tie, 0))

    m1 = jnp.maximum(sel[0], sel[1])
    m2 = jnp.minimum(sel[0], sel[1])
    for j in range(2, per):
        m2 = jnp.maximum(m2, jnp.minimum(m1, sel[j]))
        m1 = jnp.maximum(m1, sel[j])
    gscore = m1 + m2
    grank = jnp.zeros(shape, jnp.int32)
    lower_group = [None] + [jnp.where(grp >= kk, 1, 0) for kk in range(1, N_GROUPS)]
    for kk in range(1, N_GROUPS):
        other = pltpu.roll(gscore, kk, 0)
        grank = grank + beats(other, gscore, lower_group[kk])
    gmask = grank < TOPK_GROUPS
    sel = [jnp.where(gmask, s, MASK_VALUE) for s in sel]
    rolled = [[None] + [pltpu.roll(s, kk, 0) for kk in range(1, N_GROUPS)] for s in sel]
    weights = []
    for j in range(per):
        rank = jnp.zeros(shape, jnp.int32)
        for j2 in range(per):
            for kk in range(N_GROUPS):
                if kk == 0 and j2 == j:
                    continue
                if kk == 0:
                    rank = rank + beats(sel[j2], sel[j], j2 < j)
                else:
                    rank = rank + beats(rolled[j2][kk], sel[j], lower_group[kk])
        weights.append(jnp.where(rank < TOP_K, scores[j], 0.0))
    total = weights[0]
    for j in range(1, per):
        total = total + weights[j]
    total = jnp.sum(total, axis=0, keepdims=True)
    for j in range(per):
        w_ref[j * N_GROUPS:(j + 1) * N_GROUPS, :] = weights[j] / total * ROUTED_SCALE


def _router(logits_t, bias_col):
    e, n = logits_t.shape
    return pl.pallas_call(
        _router_kernel,
        grid=(n // ROUTER_TILE,),
        in_specs=[pl.BlockSpec((e, ROUTER_TILE), lambda i: (0, i)),
                  pl.BlockSpec((e, ROUTER_TILE), lambda i: (0, 0))],
        out_specs=pl.BlockSpec((e, ROUTER_TILE), lambda i: (0, i)),
        out_shape=jax.ShapeDtypeStruct((e, n), F32),
        compiler_params=_cparams(("parallel",)),
        name="router",
    )(logits_t, bias_col)


SEG_ALIGN = 8
TILE_ROWS = ROW_TILE * TOP_K + N_EXPERTS * SEG_ALIGN
CHUNKS_PER_BLOCK = 32
EXPERT_BLOCK = CHUNKS_PER_BLOCK * SEG_ALIGN
PACK_W = 512
ROW_W = PACK_W + HEAD_W
EXPERT_LANES = 128
HI_MASK = -65536


def _pack_pairs(x):
    hi = pltpu.bitcast(x[:, :PACK_W].astype(jnp.bfloat16).astype(F32), jnp.int32)
    lo = pltpu.bitcast(x[:, PACK_W:].astype(jnp.bfloat16).astype(F32), jnp.int32)
    return hi | lax.shift_right_logical(lo, jnp.full_like(lo, 16))


def _unpack_pairs(u):
    hi = pltpu.bitcast(u & HI_MASK, F32)
    lo = pltpu.bitcast(lax.shift_left(u, jnp.full_like(u, 16)), F32)
    return jnp.concatenate([hi, lo], axis=1).astype(BF16)


def _ceil_seg(c):
    return jnp.floor((c + (SEG_ALIGN - 1.0)) * (1.0 / SEG_ALIGN)) * SEG_ALIGN


def _split_256(c):
    hi = jnp.floor(c * (1.0 / 256.0))
    return hi.astype(BF16), (c - 256.0 * hi).astype(BF16)


def _moe_tables(tm):
    t = np.arange(tm)
    e = np.arange(EXPERT_LANES)
    incl_upper = (t[:, None] <= t[None, :]).astype(np.float32)
    strict_upper = (e[:, None] < e[None, :]).astype(np.float32)
    return incl_upper, strict_upper


def _dispatch_kernel(h_ref, wt_ref, wtok_ref, iu_ref, su_ref, o_ref):
    last = pl.num_programs(0) - 1

    @pl.when(pl.program_id(0) == last)
    def _():
        o_ref[...] = jnp.zeros_like(o_ref)

    @pl.when(pl.program_id(0) < last)
    def _():
        tm = h_ref.shape[0]
        r1 = o_ref.shape[0]
        sel = jnp.where(wt_ref[...] > 0.0, 1.0, 0.0)
        sel = jnp.concatenate([sel, jnp.zeros((EXPERT_LANES - N_EXPERTS, tm), F32)], axis=0)
        chosen = sel > 0.5
        mask = sel.astype(BF16)
        cnt = _dot(mask, iu_ref[...])
        ones_lanes = jnp.ones((tm, EXPERT_LANES), BF16)
        len_col = _ceil_seg(_dot(mask, ones_lanes))
        off_col = _dot_tn(su_ref[...], len_col.astype(BF16))
        len_row = _ceil_seg(_dot_nt(jnp.ones((8, tm), BF16), mask))
        off_row = _dot(len_row.astype(BF16), su_ref[...])
        dest = jnp.where(chosen, cnt + jnp.concatenate([off_col] * (tm // EXPERT_LANES), axis=1) - 1.0, -1.0)
        dhi, dlo = _split_256(dest)
        rows_e = lax.broadcasted_iota(jnp.int32, (r1, EXPERT_LANES), 0).astype(F32)
        lo = off_row[0:1, :]
        seg = jnp.where(rows_e >= lo, jnp.where(rows_e < lo + len_row[0:1, :], 1.0, 0.0), 0.0)
        segb = seg.astype(BF16)
        want = 256.0 * _dot(segb, dhi) + _dot(segb, dlo)
        rows_t = lax.broadcasted_iota(jnp.int32, (r1, tm), 0).astype(F32)
        onehot = jnp.where(want == rows_t, 1.0, 0.0).astype(BF16)
        xs = _dot(onehot, h_ref[...])
        wtok = wtok_ref[...]
        w_hi = wtok.astype(BF16)
        w_lo = (wtok - w_hi.astype(F32)).astype(BF16)
        w_row = jnp.sum((_dot(onehot, w_hi) + _dot(onehot, w_lo)) * seg, axis=1, keepdims=True)
        o_ref[:, 0:PACK_W] = _pack_pairs(xs)
        o_ref[:, PACK_W:ROW_W] = pltpu.bitcast(jnp.broadcast_to(w_row, (r1, HEAD_W)), jnp.int32)


def _dispatch(h2, w_t, w_tok):
    n, d = h2.shape
    tiles = n // ROW_TILE
    iu, su = _moe_tables(ROW_TILE)
    last = tiles - 1
    return pl.pallas_call(
        _dispatch_kernel,
        grid=(tiles + 1,),
        in_specs=[pl.BlockSpec((ROW_TILE, d), lambda i: (jnp.minimum(i, last), 0)),
                  pl.BlockSpec((N_EXPERTS, ROW_TILE), lambda i: (0, jnp.minimum(i, last))),
                  pl.BlockSpec((ROW_TILE, EXPERT_LANES), lambda i: (jnp.minimum(i, last), 0)),
                  pl.BlockSpec((ROW_TILE, ROW_TILE), lambda i: (0, 0)),
                  pl.BlockSpec((EXPERT_LANES, EXPERT_LANES), lambda i: (0, 0))],
        out_specs=pl.BlockSpec((TILE_ROWS, ROW_W), lambda i: (i, 0)),
        out_shape=jax.ShapeDtypeStruct(((tiles + 1) * TILE_ROWS, ROW_W), jnp.int32),
        compiler_params=_cparams(("parallel",)),
        name="moe_dispatch",
    )(h2, w_t, w_tok, jnp.asarray(iu, BF16), jnp.asarray(su, BF16))


def _moe_plan(w_t, tiles):
    cb = CHUNKS_PER_BLOCK
    cnt = jnp.sum((w_t > 0.0).reshape(N_EXPERTS, tiles, ROW_TILE), axis=-1, dtype=jnp.int32)
    nch = (cnt + SEG_ALIGN - 1) // SEG_ALIGN
    tile_chunk0 = jnp.cumsum(nch, axis=0) - nch
    per_expert = jnp.sum(nch, axis=1)
    padded = (per_expert + cb - 1) // cb * cb
    expert_end = jnp.cumsum(padded)
    expert_start = expert_end - padded
    seg_start = (expert_start[:, None] + jnp.cumsum(nch, axis=1) - nch).reshape(-1)
    seg_len = nch.reshape(-1)
    seg_src = (jnp.arange(tiles, dtype=jnp.int32)[None, :] * (TILE_ROWS // SEG_ALIGN) + tile_chunk0).reshape(-1)
    n_tok = tiles * ROW_TILE
    max_chunks = n_tok * TOP_K // SEG_ALIGN + tiles * N_EXPERTS + N_EXPERTS * (cb - 1)
    n_blocks = -(-max_chunks // cb)
    per_seg = jnp.stack([seg_start, seg_len, seg_src], axis=1)
    delta = per_seg - jnp.concatenate([jnp.zeros((1, 3), jnp.int32), per_seg[:-1]], axis=0)
    filled = jnp.cumsum(jnp.zeros((n_blocks * cb, 3), jnp.int32).at[seg_start].add(delta, mode='drop'), axis=0)
    c = jnp.arange(n_blocks * cb, dtype=jnp.int32) - filled[:, 0]
    used = c < filled[:, 1]
    spare0 = tiles * (TILE_ROWS // SEG_ALIGN)
    k = jnp.arange(n_blocks * cb, dtype=jnp.int32)
    read_src = jnp.where(used, filled[:, 2] + c, spare0 + 2 * cb).astype(jnp.int32)
    write_dst = jnp.where(used, filled[:, 2] + c, spare0 + k % (2 * cb)).astype(jnp.int32)
    blk = jnp.arange(n_blocks, dtype=jnp.int32) * cb
    blk_e = jnp.minimum(jnp.sum((expert_end[None, :] <= blk[:, None]).astype(jnp.int32), axis=1), N_EXPERTS - 1)
    n_active = (expert_end[-1] // cb).astype(jnp.int32).reshape(1)
    return blk_e, n_active, read_src, write_dst


def _expert_kernel(blk_e_ref, nact_ref, rsrc_ref, wdst_ref, xs_hbm, wg_ref, wu_ref, wd_ref, out_hbm,
                   xbuf, obuf, gsem, ssem):
    del blk_e_ref
    b = pl.program_id(0)
    n_active = nact_ref[0]
    cb = CHUNKS_PER_BLOCK

    def chunk_copies(blk, slot, gather, start):
        for k in range(cb):
            local = pl.ds(k * SEG_ALIGN, SEG_ALIGN)
            if gather:
                ch = rsrc_ref[blk * cb + k]
                rows = pl.ds(pl.multiple_of(ch * SEG_ALIGN, SEG_ALIGN), SEG_ALIGN)
                cp = pltpu.make_async_copy(xs_hbm.at[rows], xbuf.at[slot, local], gsem.at[slot])
            else:
                ch = wdst_ref[blk * cb + k]
                rows = pl.ds(pl.multiple_of(ch * SEG_ALIGN, SEG_ALIGN), SEG_ALIGN)
                cp = pltpu.make_async_copy(obuf.at[slot, local], out_hbm.at[rows], ssem.at[slot])
            if start:
                cp.start()
            else:
                cp.wait()

    @pl.when(b == 0)
    def _():
        chunk_copies(0, 0, True, True)

    @pl.when(b < n_active)
    def _():
        slot = b % 2

        @pl.when(b + 1 < n_active)
        def _():
            chunk_copies(b + 1, 1 - slot, True, True)

        chunk_copies(b, slot, True, False)

        @pl.when(b >= 2)
        def _():
            chunk_copies(b - 2, slot, False, False)

        x = xbuf[slot]
        xb = _unpack_pairs(x[:, 0:PACK_W])
        w = pltpu.bitcast(x[:, PACK_W:ROW_W], F32)
        hid = _silu(_dot(xb, wg_ref[0])) * _dot(xb, wu_ref[0]) * jnp.concatenate([w, w], axis=1)
        obuf[slot, :, 0:PACK_W] = _pack_pairs(_dot(hid.astype(BF16), wd_ref[0]))
        obuf[slot, :, PACK_W:ROW_W] = x[:, PACK_W:ROW_W]
        chunk_copies(b, slot, False, True)

        @pl.when(b == n_active - 1)
        def _():
            chunk_copies(b, slot, False, False)

            @pl.when(b >= 1)
            def _():
                chunk_copies(b - 1, 1 - slot, False, False)


def _experts(xs, plan, wg, wu, wd):
    blk_e, n_active, read_src, write_dst = plan
    n_blocks = blk_e.shape[0]
    d = wg.shape[1]
    wspec = lambda shape: pl.BlockSpec((1,) + shape, lambda b, be, na, rs, wd_: (be[b], 0, 0))
    return pl.pallas_call(
        _expert_kernel,
        grid_spec=pltpu.PrefetchScalarGridSpec(
            num_scalar_prefetch=4,
            grid=(n_blocks,),
            in_specs=[pl.BlockSpec(memory_space=pl.ANY),
                      wspec((d, EXPERT_DIM)), wspec((d, EXPERT_DIM)), wspec((EXPERT_DIM, d))],
            out_specs=pl.BlockSpec(memory_space=pl.ANY),
            scratch_shapes=[pltpu.VMEM((2, EXPERT_BLOCK, ROW_W), jnp.int32),
                            pltpu.VMEM((2, EXPERT_BLOCK, ROW_W), jnp.int32),
                            pltpu.SemaphoreType.DMA((2,)), pltpu.SemaphoreType.DMA((2,))]),
        out_shape=jax.ShapeDtypeStruct(xs.shape, xs.dtype),
        input_output_aliases={4: 0},
        compiler_params=_cparams(("arbitrary",)),
        name="moe_experts",
    )(blk_e, n_active, read_src, write_dst, xs, wg, wu, wd)


def _combine_kernel(ys_ref, wtok_ref, h_ref, x1_ref, mod_ref, ln_ref, il_ref, su_ref,
                    sg_ref, su2_ref, sd_ref, o_ref, *, alpha):
    tm = h_ref.shape[0]
    r1 = ys_ref.shape[0]
    chosen = wtok_ref[...] > 0.0
    mask = jnp.where(chosen, 1.0, 0.0).astype(BF16)
    cnt = _dot(il_ref[...], mask)
    len_row = _ceil_seg(_dot(jnp.ones((8, tm), BF16), mask))
    off_row = _dot(len_row.astype(BF16), su_ref[...])
    dhi, dlo = _split_256(jnp.where(chosen, cnt + off_row[0:1, :] - 1.0, -1.0))
    len_col = _ceil_seg(_dot_tn(mask, jnp.ones((tm, EXPERT_LANES), BF16)))
    off_col = _dot_tn(su_ref[...], len_col.astype(BF16))
    reps = r1 // EXPERT_LANES
    lo = jnp.concatenate([off_col] * reps, axis=1)
    hi = lo + jnp.concatenate([len_col] * reps, axis=1)
    rows_e = lax.broadcasted_iota(jnp.int32, (EXPERT_LANES, r1), 1).astype(F32)
    segb = jnp.where(rows_e >= lo, jnp.where(rows_e < hi, 1.0, 0.0), 0.0).astype(BF16)
    want = 256.0 * _dot(dhi, segb) + _dot(dlo, segb)
    rows_t = lax.broadcasted_iota(jnp.int32, (tm, r1), 1).astype(F32)
    onehot = jnp.where(want == rows_t, 1.0, 0.0).astype(BF16)
    y = _dot(onehot, _unpack_pairs(ys_ref[:, 0:PACK_W]))
    x = h_ref[...]
    y = y + _dot((_silu(_dot(x, sg_ref[...])) * _dot(x, su2_ref[...])).astype(BF16), sd_ref[...])
    gate2 = mod_ref[0, 0, 5:6, :]
    o_ref[...] = _ln(alpha * x1_ref[...] + gate2 * y) * ln_ref[0:1, :] + ln_ref[1:2, :]


def _combine(ys, w_tok, h2, x1, mods, ln2, sg, su, sd, alpha, tiles_per_seq, n_ctx_tiles):
    n, d = h2.shape
    tiles = n // ROW_TILE
    iu, sup = _moe_tables(ROW_TILE)
    full = lambda arr: pl.BlockSpec(arr.shape, lambda i: (0,) * arr.ndim)
    row = pl.BlockSpec((ROW_TILE, d), lambda i: (i, 0))
    il = jnp.asarray(iu.T, BF16)
    sup = jnp.asarray(sup, BF16)
    return pl.pallas_call(
        functools.partial(_combine_kernel, alpha=alpha),
        grid=(tiles,),
        in_specs=[pl.BlockSpec((TILE_ROWS, ROW_W), lambda i: (i, 0)),
                  pl.BlockSpec((ROW_TILE, EXPERT_LANES), lambda i: (i, 0)),
                  row, row,
                  pl.BlockSpec((1, 1, 6, d), lambda i: (i // tiles_per_seq,
                                                        (i % tiles_per_seq >= n_ctx_tiles).astype(jnp.int32), 0, 0)),
                  full(ln2), full(il), full(sup), full(sg), full(su), full(sd)],
        out_specs=row,
        out_shape=jax.ShapeDtypeStruct((n, d), F32),
        compiler_params=_cparams(("parallel",)),
        name="moe_combine",
    )(ys, w_tok, h2, x1, mods, ln2, il, sup, sg, su, sd)


def _rope_tables(n_lat, n_ctx):
    rows = n_lat // GRID_W
    row = jnp.repeat(jnp.arange(rows, dtype=F32), GRID_W)
    col = (jnp.arange(n_lat) % GRID_W).astype(F32)
    n_freq = DA_DH // 4
    inv = ROPE_THETA ** (-jnp.arange(n_freq, dtype=F32) / n_freq)
    ang = jnp.concatenate([row[:, None] * inv, col[:, None] * inv], axis=-1)
    ang = jnp.concatenate([ang, ang], axis=-1)
    cos, sin = jnp.cos(ang), jnp.sin(ang)
    sign = jnp.where(jnp.arange(DA_DH) < DA_DH // 2, -1.0, 1.0).astype(F32)
    cos = jnp.tile(cos, (1, HEAD_W // DA_DH))
    sin = jnp.tile(sin * sign, (1, HEAD_W // DA_DH))
    cos = jnp.concatenate([jnp.ones((n_ctx, HEAD_W), F32), cos], axis=0)
    sin = jnp.concatenate([jnp.zeros((n_ctx, HEAD_W), F32), sin], axis=0)
    return cos, sin


def kernel(x, c, ctx, c_ctx, w_ada, b_ada, w_in, lb_logits, hg_norm_g, lam_params, da_norm_g, w_branch_a, w_branch_b, w_branch_c, w_out, ln1_g, ln1_b, ln2_g, ln2_b, w_router, router_bias, w_exp_gate, w_exp_up, w_exp_down, w_sh_gate, w_sh_up, w_sh_down):
    bsz, n_lat, d = x.shape
    n_ctx = ctx.shape[1]
    depth = w_in.shape[0]
    t = n_ctx + n_lat
    assert n_ctx % ROW_TILE == 0 and n_lat % ROW_TILE == 0
    assert n_ctx % HG_CHUNK == 0 and n_lat % HG_CHUNK == 0 and n_lat % GRID_W == 0
    n_ctx_tiles = n_ctx // ROW_TILE
    alpha = (2 * depth) ** 0.25
    per = N_EXPERTS // N_GROUPS

    cos, sin = _rope_tables(n_lat, n_ctx)
    sm = jax.nn.softmax(lb_logits.astype(F32), axis=0)
    lower = jnp.cumsum(sm, axis=0) - sm[0:1]
    log_lb = jnp.log(jnp.maximum(lower, LB_FLOOR))
    log1m_lb = jnp.log1p(-lower)

    n_rows = -(-(bsz + 1) // 8) * 8
    cvec = jnp.concatenate([c, c_ctx[None, :], jnp.zeros((n_rows - bsz - 1, d), F32)], axis=0)

    n_branch_cols = w_in.shape[2] - 3 * d

    def member_major(a):
        return a.reshape((N_GROUPS, per) + a.shape[1:]).swapaxes(0, 1).reshape(a.shape)

    xs = jnp.concatenate([ctx, x], axis=1)
    for l in range(depth):
        mod = _ada(cvec, w_ada[l], b_ada[l])
        mod_lat = mod[:bsz].reshape(bsz, 1, 6, d)
        mod_ctx = jnp.broadcast_to(mod[bsz].reshape(1, 1, 6, d), (bsz, 1, 6, d))
        mods = jnp.concatenate([mod_ctx, mod_lat], axis=1)

        w_in_l = jnp.concatenate([w_in[l][:, n_branch_cols:], w_in[l][:, :n_branch_cols]], axis=1)
        p = _inproj(xs, mods, w_in_l.astype(BF16), n_ctx)
        lb_f = jnp.stack([log_lb[l, 0], log1m_lb[l, 0]])
        lb_b = jnp.stack([log_lb[l, 1], log1m_lb[l, 1]])
        o_f, o_b = _hgrn(p, lb_f, lb_b, n_ctx)

        lam_init = 0.8 - 0.6 * math.exp(-0.3 * l)
        lp = lam_params[l].astype(F32)
        lam = jnp.exp(jnp.sum(lp[0] * lp[1])) - jnp.exp(jnp.sum(lp[2] * lp[3])) + lam_init
        qr, kr, vr = _attn_prep(p, cos, sin)
        at = _attention(qr, kr, vr, jnp.full((8, HEAD_W), lam, F32), n_ctx)
        fr = _fourier(p, n_ctx)

        ln1 = jnp.stack([ln1_g[l], ln1_b[l]])
        hg_g = hg_norm_g[l].reshape(1, HEAD_W)
        da_g = da_norm_g[l].reshape(1, HEAD_W)
        x1, h2, logits_t = _merge(
            xs, mods, p, o_f, o_b, at, fr, hg_g, da_g,
            w_branch_a[l].astype(BF16), w_branch_b[l].astype(BF16), w_branch_c[l].astype(BF16),
            w_out[l].astype(BF16), ln1, member_major(w_router[l].T), alpha, 1.0 - lam_init, n_ctx_tiles)

        bias_col = jnp.broadcast_to(member_major(router_bias[l])[:, None], (N_EXPERTS, ROUTER_TILE))
        wts_t = _router(logits_t, bias_col)
        n_tok = bsz * t
        w_tok = jnp.concatenate([wts_t.T, jnp.zeros((n_tok, EXPERT_LANES - N_EXPERTS), F32)], axis=1)
        h2f = h2.reshape(n_tok, d)
        rows = _dispatch(h2f, wts_t, w_tok)
        plan = _moe_plan(wts_t, n_tok // ROW_TILE)
        rows = _experts(rows, plan, member_major(w_exp_gate[l]).astype(BF16),
                        member_major(w_exp_up[l]).astype(BF16), member_major(w_exp_down[l]).astype(BF16))
        ln2 = jnp.stack([ln2_g[l], ln2_b[l]])
        xs = _combine(rows, w_tok, h2f, x1.reshape(n_tok, d), mods, ln2,
                      w_sh_gate[l].astype(BF16), w_sh_up[l].astype(BF16), w_sh_down[l].astype(BF16),
                      alpha, t // ROW_TILE, n_ctx_tiles).reshape(bsz, t, d)
    return xs[:, n_ctx:]
```

```python
import functools
import math

import numpy as np
import jax
import jax.numpy as jnp
from jax import lax
from jax.experimental import pallas as pl
from jax.experimental.pallas import tpu as pltpu

F32 = jnp.float32
BF16 = jnp.bfloat16

HG_HEADS = 4
HG_DK = 128
LB_FLOOR = 1e-30
DA_HEADS = 4
DA_DH = 64
ROPE_THETA = 10000.0
GRID_W = 64
FR_GROUPS = 4
FR_DG = 128
N_EXPERTS = 64
TOP_K = 8
EXPERT_DIM = 256
N_GROUPS = 8
TOPK_GROUPS = 4
ROUTED_SCALE = 2.5
MASK_VALUE = -1e9
LN_EPS = 1e-5
LOG2_E = 1.4426950408889634

HEAD_W = 128
BRANCH_W = 512
GATES_BLOCKS = 6
COL_Q, COL_FF, COL_FB, COL_I, COL_OG, COL_QA, COL_KA, COL_VA, COL_U = range(6, 15)

ROW_TILE = 256
HG_CHUNK = 128
HG_LEVELS = (64, 32, 16, 8, 4, 2, 1)
HG_ROW_REF_MIN = 8
IN_N_TILE = 2560
IN_ROW_TILE = 768
ROUTER_TILE = 512
VMEM_LIMIT = 48 * 1024 * 1024


def _cparams(sem):
    return pltpu.CompilerParams(dimension_semantics=sem, vmem_limit_bytes=VMEM_LIMIT)


def _dot(a, b):
    return jnp.dot(a, b, preferred_element_type=F32)


def _dot_nt(a, b):
    return lax.dot_general(a, b, (((1,), (1,)), ((), ())), preferred_element_type=F32)


def _dot_tn(a, b):
    return lax.dot_general(a, b, (((0,), (0,)), ((), ())), preferred_element_type=F32)


def _silu(x):
    return x * jax.nn.sigmoid(x)


def _ln(x):
    mu = jnp.mean(x, axis=-1, keepdims=True)
    xc = x - mu
    var = jnp.mean(xc * xc, axis=-1, keepdims=True)
    return xc * lax.rsqrt(var + LN_EPS)


def _ada_kernel(c_ref, w_ref, b_ref, o_ref):
    o_ref[...] = jnp.dot(_silu(c_ref[...]), w_ref[...], preferred_element_type=F32,
                         precision=lax.Precision.HIGHEST) + b_ref[...]


def _ada(cvec, w, b):
    rows, d = cvec.shape
    n = w.shape[1]
    tn = 1536
    return pl.pallas_call(
        _ada_kernel,
        grid=(n // tn,),
        in_specs=[pl.BlockSpec((rows, d), lambda j: (0, 0)),
                  pl.BlockSpec((d, tn), lambda j: (0, j)),
                  pl.BlockSpec((1, tn), lambda j: (0, j))],
        out_specs=pl.BlockSpec((rows, tn), lambda j: (0, j)),
        out_shape=jax.ShapeDtypeStruct((rows, n), F32),
        compiler_params=_cparams(("parallel",)),
        name="ada",
    )(cvec, w, b.reshape(1, n))


def _inproj_kernel(x_ref, mod_ref, w_ref, o_ref, *, n_ctx):
    first = pl.program_id(2) == 0
    xn = _ln(x_ref[0])
    shift_top = jnp.where(first, mod_ref[0, 0, 0:1, :], mod_ref[0, 1, 0:1, :])
    scale_top = jnp.where(first, mod_ref[0, 0, 1:2, :], mod_ref[0, 1, 1:2, :])
    parts = [xn[:n_ctx] * (1.0 + scale_top) + shift_top]
    if xn.shape[0] > n_ctx:
        parts.append(xn[n_ctx:] * (1.0 + mod_ref[0, 1, 1:2, :]) + mod_ref[0, 1, 0:1, :])
    h = jnp.concatenate(parts, axis=0) if len(parts) > 1 else parts[0]
    o_ref[0] = _dot(h.astype(BF16), w_ref[...])


def _inproj(x, mods, w, n_ctx):
    bsz, t, d = x.shape
    n = w.shape[1]
    tm = IN_ROW_TILE if t % IN_ROW_TILE == 0 and n_ctx <= IN_ROW_TILE else n_ctx
    return pl.pallas_call(
        functools.partial(_inproj_kernel, n_ctx=n_ctx),
        grid=(n // IN_N_TILE, bsz, t // tm),
        in_specs=[pl.BlockSpec((1, tm, d), lambda j, b, i: (b, i, 0)),
                  pl.BlockSpec((1, 2, 6, d), lambda j, b, i: (b, 0, 0, 0)),
                  pl.BlockSpec((d, IN_N_TILE), lambda j, b, i: (0, j))],
        out_specs=pl.BlockSpec((1, tm, IN_N_TILE), lambda j, b, i: (b, i, j)),
        out_shape=jax.ShapeDtypeStruct((bsz, t, n), F32),
        compiler_params=_cparams(("parallel", "parallel", "parallel")),
        name="inproj",
    )(x, mods, w)


def _hgrn_tables(reverse):
    c = HG_CHUNK
    t = np.arange(c)
    if reverse:
        cum = (t[None, :] >= t[:, None])
    else:
        cum = (t[None, :] <= t[:, None])
    mats = [cum]
    for m in HG_LEVELS:
        if m < HG_ROW_REF_MIN:
            idx = (t // (2 * m)) * (2 * m) + m - (1 if reverse else 0)
            mats.append(cum[idx])
    stack = np.concatenate(mats, axis=0).astype(np.float32)
    x = t[:, None] ^ t[None, :]
    lvl = np.zeros((c, c), np.int32)
    for li, m in enumerate(HG_LEVELS):
        lvl = np.where((x >= m) & (x < 2 * m), li + 1, lvl)
    visible = (t[None, :] >= t[:, None]) if reverse else (t[None, :] <= t[:, None])
    lvl = np.where(visible, lvl, -1).astype(np.int32)
    return stack, lvl


def _hgrn_kernel(qf_ref, ff_ref, vf_ref, qb_ref, fb_ref, vb_ref, lbf_ref, lbb_ref,
                 stf_ref, lvf_ref, stb_ref, lvb_ref, of_ref, ob_ref, sf_ref, sb_ref):
    c = HG_CHUNK

    @pl.when(pl.program_id(1) == 0)
    def _():
        sf_ref[...] = jnp.zeros_like(sf_ref)
        sb_ref[...] = jnp.zeros_like(sb_ref)

    chains = []
    for h in range(HG_HEADS):
        chains.append((qf_ref, ff_ref, vf_ref, lbf_ref, stf_ref, lvf_ref, of_ref, sf_ref, h, False))
        chains.append((qb_ref, fb_ref, vb_ref, lbb_ref, stb_ref, lvb_ref, ob_ref, sb_ref, h, True))

    gates = []
    for q_ref, f_ref, v_ref, lb_ref, _, _, _, _, h, _ in chains:
        sl = slice(h * HEAD_W, (h + 1) * HEAD_W)
        q = _silu(q_ref[0, :, sl])
        fr = f_ref[0, :, sl]
        v = v_ref[0, :, sl].astype(BF16)
        log_lb = lb_ref[0:1, sl]
        log1m_lb = lb_ref[1:2, sl]
        c2 = log1m_lb + jnp.minimum(fr, 0.0) - jnp.log1p(jnp.exp(-jnp.abs(fr)))
        g = jnp.maximum(log_lb, c2) + jnp.log1p(jnp.exp(-jnp.abs(log_lb - c2)))
        gates.append((q, 1.0 - jnp.exp(g), v, g))

    cums = []
    for (_, _, _, _, stack_ref, _, _, _, _, _), (_, _, _, g) in zip(chains, gates):
        g1 = g.astype(BF16)
        r1 = g - g1.astype(F32)
        g2 = r1.astype(BF16)
        g3 = (r1 - g2.astype(F32)).astype(BF16)
        stack = stack_ref[...]
        cums.append(_dot(stack, g1) + _dot(stack, g2) + _dot(stack, g3))

    acc = []
    for (_, _, _, _, _, lvl_ref, _, _, _, _), (q, k, _, _) in zip(chains, gates):
        acc.append(jnp.where(lvl_ref[...] == 0, _dot_nt(q.astype(BF16), k.astype(BF16)), 0.0))
    n_stacked = 0
    for li, m in enumerate(HG_LEVELS):
        if m < HG_ROW_REF_MIN:
            n_stacked += 1
        for ci, (chain, (q, k, _, _), bb) in enumerate(zip(chains, gates, cums)):
            reverse = chain[9]
            b = bb[0:c]
            if m >= HG_ROW_REF_MIN:
                rows = []
                for j in range(c // (2 * m)):
                    idx = j * 2 * m + m - (1 if reverse else 0)
                    rows.append(jnp.broadcast_to(b[idx:idx + 1, :], (2 * m, HEAD_W)))
                ref = rows[0] if len(rows) == 1 else jnp.concatenate(rows, axis=0)
            else:
                ref = bb[n_stacked * c:(n_stacked + 1) * c]
            e = jnp.exp(-jnp.abs(b - ref))
            acc[ci] = jnp.where(chain[5][...] == li + 1,
                                _dot_nt((q * e).astype(BF16), (k * e).astype(BF16)), acc[ci])

    for chain, (q, k, v, _), bb, a in zip(chains, gates, cums, acc):
        o_ref, st_ref, h, reverse = chain[6], chain[7], chain[8], chain[9]
        sl = slice(h * HEAD_W, (h + 1) * HEAD_W)
        b = bb[0:c]
        st = st_ref[h]
        o_ref[0, :, sl] = _dot(a.astype(BF16), v) + _dot_nt((q * jnp.exp(b)).astype(BF16), st.astype(BF16))
        b_end = b[0:1] if reverse else b[c - 1:c]
        k_end = (k * jnp.exp(b_end - b)).astype(BF16)
        st_ref[h] = st * jnp.exp(b_end) + _dot_tn(v, k_end)


def _hgrn(p, lb_f, lb_b, n_ctx):
    bsz, t, _ = p.shape
    c = HG_CHUNK
    nc, nt = n_ctx // c, t // c
    stack_f, lvl_f = _hgrn_tables(False)
    stack_b, lvl_b = _hgrn_tables(True)

    def rev(j):
        return jnp.where(j < nc, nc - 1 - j, nc + nt - 1 - j)

    nst = stack_f.shape[0]
    blk = lambda row, col: pl.BlockSpec((1, c, BRANCH_W), lambda b, j: (b, row(j), col))
    const = lambda shape: pl.BlockSpec(shape, lambda b, j: (0, 0))
    fwd = lambda j: j
    out = jax.ShapeDtypeStruct((bsz, t, BRANCH_W), F32)
    state = pltpu.VMEM((HG_HEADS, HEAD_W, HEAD_W), F32)
    return pl.pallas_call(
        _hgrn_kernel,
        grid=(bsz, nt),
        in_specs=[blk(fwd, COL_Q), blk(fwd, COL_FF), blk(fwd, COL_I),
                  blk(rev, COL_Q), blk(rev, COL_FB), blk(rev, COL_I),
                  const((2, BRANCH_W)), const((2, BRANCH_W)),
                  const((nst, c)), const((c, c)), const((nst, c)), const((c, c))],
        out_specs=[blk(fwd, 0), blk(rev, 0)],
        out_shape=[out, out],
        scratch_shapes=[state, state],
        compiler_params=_cparams(("parallel", "arbitrary")),
        name="hgrn",
    )(p, p, p, p, p, p, lb_f, lb_b, jnp.asarray(stack_f, BF16), jnp.asarray(lvl_f),
      jnp.asarray(stack_b, BF16), jnp.asarray(lvl_b))


def _attn_prep_kernel(q_ref, k_ref, v_ref, cos_ref, sin_ref, qo_ref, ko_ref, vo_ref):
    cos = cos_ref[...]
    sin = sin_ref[...]
    lane = lax.broadcasted_iota(jnp.int32, cos.shape, 1)
    first_half = (lane % DA_DH) < (DA_DH // 2)

    def rope(x):
        up = pltpu.roll(x, HEAD_W - DA_DH // 2, 1)
        down = pltpu.roll(x, DA_DH // 2, 1)
        return x * cos + jnp.where(first_half, up, down) * sin

    ones = jnp.ones((q_ref.shape[1], HEAD_W), BF16)
    for h in range(DA_HEADS):
        sl = slice(h * HEAD_W, (h + 1) * HEAD_W)
        qo_ref[0, :, sl] = (rope(q_ref[0, :, sl]) * (DA_DH ** -0.5 * LOG2_E)).astype(BF16)
        ko_ref[0, :, sl] = rope(k_ref[0, :, sl]).astype(BF16)
        vo_ref[0, :, 2 * h * HEAD_W:(2 * h + 1) * HEAD_W] = v_ref[0, :, sl].astype(BF16)
        vo_ref[0, :, (2 * h + 1) * HEAD_W:(2 * h + 2) * HEAD_W] = ones


def _attn_prep(p, cos, sin):
    bsz, t, _ = p.shape
    spec = lambda col: pl.BlockSpec((1, ROW_TILE, BRANCH_W), lambda b, i: (b, i, col))
    tab = pl.BlockSpec((ROW_TILE, HEAD_W), lambda b, i: (i, 0))
    out = jax.ShapeDtypeStruct((bsz, t, BRANCH_W), BF16)
    return pl.pallas_call(
        _attn_prep_kernel,
        grid=(bsz, t // ROW_TILE),
        in_specs=[spec(COL_QA), spec(COL_KA), spec(COL_VA), tab, tab],
        out_specs=[spec(0), spec(0), pl.BlockSpec((1, ROW_TILE, 2 * BRANCH_W), lambda b, i: (b, i, 0))],
        out_shape=[out, out, jax.ShapeDtypeStruct((bsz, t, 2 * BRANCH_W), BF16)],
        compiler_params=_cparams(("parallel", "parallel")),
        name="attn_prep",
    )(p, p, p, cos, sin)


def _attn_kernel(q_ref, k_ref, v_ref, lam_ref, o_ref, *, n_ctx, n_ctx_tiles):
    lam = lam_ref[0:1, 0:1]

    def attend(n_keys):
        k = k_ref[0, 0:n_keys, :]
        v = v_ref[0, 0:n_keys, :]
        half = q_ref.shape[1] // 2
        lane = lax.broadcasted_iota(jnp.int32, (half, HEAD_W), 1)
        chains = [(r, first) for r in range(2) for first in (True, False)]
        scores = []
        for r, first in chains:
            q = q_ref[0, r * half:(r + 1) * half, :]
            qm = jnp.where((lane < DA_DH) == first, q, jnp.zeros_like(q))
            scores.append(_dot_nt(qm, k))
        probs = [jnp.exp2(s - jnp.max(s, axis=-1, keepdims=True)).astype(BF16) for s in scores]
        outs = []
        for e in probs:
            pv = _dot(e, v)
            outs.append(pv[:, 0:HEAD_W] / pv[:, HEAD_W:2 * HEAD_W])
        for r in range(2):
            o_ref[0, r * half:(r + 1) * half, :] = outs[2 * r] - lam * outs[2 * r + 1]

    is_ctx = pl.program_id(2) < n_ctx_tiles

    @pl.when(is_ctx)
    def _():
        attend(n_ctx)

    @pl.when(jnp.logical_not(is_ctx))
    def _():
        attend(k_ref.shape[1])


def _attention(q, k, v, lam, n_ctx):
    bsz, t, _ = q.shape
    n_ctx_tiles = n_ctx // ROW_TILE
    return pl.pallas_call(
        functools.partial(_attn_kernel, n_ctx=n_ctx, n_ctx_tiles=n_ctx_tiles),
        grid=(bsz, DA_HEADS, t // ROW_TILE),
        in_specs=[pl.BlockSpec((1, ROW_TILE, HEAD_W), lambda b, h, i: (b, i, h)),
                  pl.BlockSpec((1, t, HEAD_W), lambda b, h, i: (b, 0, h)),
                  pl.BlockSpec((1, t, 2 * HEAD_W), lambda b, h, i: (b, 0, h)),
                  pl.BlockSpec((8, HEAD_W), lambda b, h, i: (0, 0))],
        out_specs=pl.BlockSpec((1, ROW_TILE, HEAD_W), lambda b, h, i: (b, i, h)),
        out_shape=jax.ShapeDtypeStruct((bsz, t, BRANCH_W), F32),
        compiler_params=_cparams(("parallel", "parallel", "parallel")),
        name="diff_attn",
    )(q, k, v, lam)


def _dft_tables(n):
    idx = np.arange(n, dtype=np.int64)
    ang = 2.0 * np.pi * ((idx[:, None] * idx[None, :]) % n).astype(np.float64) / n
    scale = 1.0 / math.sqrt(n * FR_DG)
    return np.concatenate([np.cos(ang), -np.sin(ang)], axis=1).astype(np.float32) * np.float32(scale)


def _channel_dft_tables():
    idx = np.arange(FR_DG, dtype=np.int64)
    ang = 2.0 * np.pi * ((idx[:, None] * idx[None, :]) % FR_DG).astype(np.float64) / FR_DG
    eye = np.eye(FR_GROUPS)
    return (np.kron(eye, np.cos(ang)).astype(np.float32), np.kron(eye, np.sin(ang)).astype(np.float32))


def _fourier_kernel(u_ref, cc_ref, sc_ref, wc_ref, wl_ref, o_ref, zc_ref, zl_ref, *, n_ctx, n_ctx_tiles):
    i = pl.program_id(1)
    n_lat = u_ref.shape[1] - n_ctx

    @pl.when(i == 0)
    def _():
        u = u_ref[0].astype(BF16)
        zcos = _dot(u, cc_ref[...]).astype(BF16)
        zsin = _dot(u, sc_ref[...]).astype(BF16)
        zc_ref[0:n_ctx] = zcos[0:n_ctx]
        zc_ref[n_ctx:2 * n_ctx] = zsin[0:n_ctx]
        zl_ref[0:n_lat] = zcos[n_ctx:]
        zl_ref[n_lat:2 * n_lat] = zsin[n_ctx:]

    @pl.when(i < n_ctx_tiles)
    def _():
        o_ref[0] = _dot(wc_ref[...], zc_ref[...])

    @pl.when(i >= n_ctx_tiles)
    def _():
        o_ref[0] = _dot(wl_ref[...], zl_ref[...])


def _fourier(p, n_ctx):
    bsz, t, _ = p.shape
    n_lat = t - n_ctx
    n_ctx_tiles = n_ctx // ROW_TILE
    cc, sc = _channel_dft_tables()
    wc = jnp.asarray(_dft_tables(n_ctx)).astype(BF16)
    wl = jnp.asarray(_dft_tables(n_lat)).astype(BF16)
    cc = jnp.asarray(cc).astype(BF16)
    sc = jnp.asarray(sc).astype(BF16)
    return pl.pallas_call(
        functools.partial(_fourier_kernel, n_ctx=n_ctx, n_ctx_tiles=n_ctx_tiles),
        grid=(bsz, t // ROW_TILE),
        in_specs=[pl.BlockSpec((1, t, BRANCH_W), lambda b, i: (b, 0, COL_U)),
                  pl.BlockSpec((BRANCH_W, BRANCH_W), lambda b, i: (0, 0)),
                  pl.BlockSpec((BRANCH_W, BRANCH_W), lambda b, i: (0, 0)),
                  pl.BlockSpec((ROW_TILE, 2 * n_ctx), lambda b, i: (jnp.minimum(i, n_ctx_tiles - 1), 0)),
                  pl.BlockSpec((ROW_TILE, 2 * n_lat), lambda b, i: (jnp.maximum(i - n_ctx_tiles, 0), 0))],
        out_specs=pl.BlockSpec((1, ROW_TILE, BRANCH_W), lambda b, i: (b, i, 0)),
        out_shape=jax.ShapeDtypeStruct((bsz, t, BRANCH_W), F32),
        scratch_shapes=[pltpu.VMEM((2 * n_ctx, BRANCH_W), BF16), pltpu.VMEM((2 * n_lat, BRANCH_W), BF16)],
        compiler_params=_cparams(("parallel", "arbitrary")),
        name="fourier",
    )(p, cc, sc, wc, wl)


def _head_rms(x, gain):
    parts = []
    for h in range(BRANCH_W // HEAD_W):
        xh = x[:, h * HEAD_W:(h + 1) * HEAD_W]
        parts.append(xh * lax.rsqrt(jnp.mean(xh * xh, axis=-1, keepdims=True) + LN_EPS) * gain)
    return jnp.concatenate(parts, axis=-1)


def _merge_kernel(x_ref, mod_ref, gates_ref, og_ref, of_ref, ob_ref, at_ref, fr_ref,
                  hg_ref, da_ref, wa_ref, wb_ref, wc_ref, wo_ref, ln_ref, wr_ref,
                  x1_ref, h2_ref, lg_ref, *, alpha, attn_scale):
    d = x_ref.shape[2]
    tm = x_ref.shape[1]
    gate1 = mod_ref[0, 0, 2:3, :]
    shift2 = mod_ref[0, 0, 3:4, :]
    scale2 = mod_ref[0, 0, 4:5, :]
    halves = [slice(r * (tm // 2), (r + 1) * (tm // 2)) for r in range(2)]
    a = [(_head_rms(of_ref[0, rs, :] + ob_ref[0, rs, :], hg_ref[...]) * _silu(og_ref[0, rs, :])).astype(BF16)
         for rs in halves]
    bq = [(_head_rms(at_ref[0, rs, :], da_ref[...]) * attn_scale).astype(BF16) for rs in halves]
    pa = [_dot(v, wa_ref[...]) for v in a]
    pb = [_dot(v, wb_ref[...]) for v in bq]
    pc = [_dot(fr_ref[0, rs, :].astype(BF16), wc_ref[...]) for rs in halves]
    m = []
    for rs, ya, yb, yc in zip(halves, pa, pb, pc):
        m.append((jax.nn.sigmoid(gates_ref[0, rs, 0:d]) * ya + jax.nn.sigmoid(gates_ref[0, rs, d:2 * d]) * yb
                  + jax.nn.sigmoid(gates_ref[0, rs, 2 * d:3 * d]) * yc).astype(BF16))
    mix = [_dot(v, wo_ref[...]) for v in m]
    h2s = []
    for rs, mx in zip(halves, mix):
        x1 = _ln(alpha * x_ref[0, rs, :] + gate1 * mx) * ln_ref[0:1, :] + ln_ref[1:2, :]
        x1_ref[0, rs, :] = x1
        h2 = _ln(x1) * (1.0 + scale2) + shift2
        h2_ref[0, rs, :] = h2.astype(BF16)
        h2s.append(h2)
    for rs, h2 in zip(halves, h2s):
        lg_ref[:, rs] = lax.dot_general(wr_ref[...], h2, (((1,), (1,)), ((), ())),
                                        preferred_element_type=F32, precision=lax.Precision.HIGHEST)


def _merge(x, mods, p, o_f, o_b, at, fr, hg_g, da_g, wa, wb, wc, wo, ln1, wr_t, alpha, attn_scale, n_ctx_tiles):
    bsz, t, d = x.shape
    tiles = t // ROW_TILE
    row = lambda w, col: pl.BlockSpec((1, ROW_TILE, w), lambda b, i: (b, i, col))
    full = lambda arr: pl.BlockSpec(arr.shape, lambda b, i: (0,) * arr.ndim)
    return pl.pallas_call(
        functools.partial(_merge_kernel, alpha=alpha, attn_scale=attn_scale),
        grid=(bsz, tiles),
        in_specs=[row(d, 0),
                  pl.BlockSpec((1, 1, 6, d), lambda b, i: (b, (i >= n_ctx_tiles).astype(jnp.int32), 0, 0)),
                  row(GATES_BLOCKS * BRANCH_W, 0), row(BRANCH_W, COL_OG),
                  row(BRANCH_W, 0), row(BRANCH_W, 0), row(BRANCH_W, 0), row(BRANCH_W, 0),
                  full(hg_g), full(da_g), full(wa), full(wb), full(wc), full(wo), full(ln1), full(wr_t)],
        out_specs=[row(d, 0), row(d, 0),
                   pl.BlockSpec((N_EXPERTS, ROW_TILE), lambda b, i: (0, b * tiles + i))],
        out_shape=[jax.ShapeDtypeStruct((bsz, t, d), F32),
                   jax.ShapeDtypeStruct((bsz, t, d), BF16),
                   jax.ShapeDtypeStruct((N_EXPERTS, bsz * t), F32)],
        compiler_params=_cparams(("parallel", "parallel")),
        name="merge",
    )(x, mods, p, p, o_f, o_b, at, fr, hg_g, da_g, wa, wb, wc, wo, ln1, wr_t)


def _router_kernel(lg_ref, bias_ref, w_ref):
    per = N_EXPERTS // N_GROUPS
    scores = [jax.nn.sigmoid(lg_ref[j * N_GROUPS:(j + 1) * N_GROUPS, :]) for j in range(per)]
    sel = [scores[j] + bias_ref[j * N_GROUPS:(j + 1) * N_GROUPS, :] for j in range(per)]
    shape = sel[0].shape
    grp = lax.broadcasted_iota(jnp.int32, shape, 0)

    def beats(other, me, tie):
        if isinstance(tie, bool):
            return jnp.where((other >= me) if tie else (other > me), 1, 0)
        return jnp.where(other > me, 1, jnp.where(other == me, tie, 0))

    m1 = jnp.maximum(sel[0], sel[1])
    m2 = jnp.minimum(sel[0], sel[1])
    for j in range(2, per):
        m2 = jnp.maximum(m2, jnp.minimum(m1, sel[j]))
        m1 = jnp.maximum(m1, sel[j])
    gscore = m1 + m2
    grank = jnp.zeros(shape, jnp.int32)
    lower_group = [None] + [jnp.where(grp >= kk, 1, 0) for kk in range(1, N_GROUPS)]
    for kk in range(1, N_GROUPS):
        other = pltpu.roll(gscore, kk, 0)
        grank = grank + beats(other, gscore, lower_group[kk])
    gmask = grank < TOPK_GROUPS
    sel = [jnp.where(gmask, s, MASK_VALUE) for s in sel]
    rolled = [[None] + [pltpu.roll(s, kk, 0) for kk in range(1, N_GROUPS)] for s in sel]
    weights = []
    for j in range(per):
        rank = jnp.zeros(shape, jnp.int32)
        for j2 in range(per):
            for kk in range(N_GROUPS):
                if kk == 0 and j2 == j:
                    continue
                if kk == 0:
                    rank = rank + beats(sel[j2], sel[j], j2 < j)
                else:
                    rank = rank + beats(rolled[j2][kk], sel[j], lower_group[kk])
        weights.append(jnp.where(rank < TOP_K, scores[j], 0.0))
    total = weights[0]
    for j in range(1, per):
        total = total + weights[j]
    total = jnp.sum(total, axis=0, keepdims=True)
    for j in range(per):
        w_ref[j * N_GROUPS:(j + 1) * N_GROUPS, :] = weights[j] / total * ROUTED_SCALE


def _router(logits_t, bias_col):
    e, n = logits_t.shape
    return pl.pallas_call(
        _router_kernel,
        grid=(n // ROUTER_TILE,),
        in_specs=[pl.BlockSpec((e, ROUTER_TILE), lambda i: (0, i)),
                  pl.BlockSpec((e, ROUTER_TILE), lambda i: (0, 0))],
        out_specs=pl.BlockSpec((e, ROUTER_TILE), lambda i: (0, i)),
        out_shape=jax.ShapeDtypeStruct((e, n), F32),
        compiler_params=_cparams(("parallel",)),
        name="router",
    )(logits_t, bias_col)


SEG_ALIGN = 8
TILE_ROWS = ROW_TILE * TOP_K + N_EXPERTS * SEG_ALIGN
CHUNKS_PER_BLOCK = 64
EXPERT_BLOCK = CHUNKS_PER_BLOCK * SEG_ALIGN
PACK_W = 512
ROW_W = PACK_W + HEAD_W
EXPERT_LANES = 128
HI_MASK = -65536


def _pack_pairs(x):
    hi = pltpu.bitcast(x[:, :PACK_W].astype(jnp.bfloat16).astype(F32), jnp.int32)
    lo = pltpu.bitcast(x[:, PACK_W:].astype(jnp.bfloat16).astype(F32), jnp.int32)
    return hi | lax.shift_right_logical(lo, jnp.full_like(lo, 16))


def _unpack_pairs(u):
    hi = pltpu.bitcast(u & HI_MASK, F32)
    lo = pltpu.bitcast(lax.shift_left(u, jnp.full_like(u, 16)), F32)
    return jnp.concatenate([hi, lo], axis=1).astype(BF16)


def _ceil_seg(c):
    return jnp.floor((c + (SEG_ALIGN - 1.0)) * (1.0 / SEG_ALIGN)) * SEG_ALIGN


def _split_256(c):
    hi = jnp.floor(c * (1.0 / 256.0))
    return hi.astype(BF16), (c - 256.0 * hi).astype(BF16)


def _moe_tables(tm):
    t = np.arange(tm)
    e = np.arange(EXPERT_LANES)
    incl_upper = (t[:, None] <= t[None, :]).astype(np.float32)
    strict_upper = (e[:, None] < e[None, :]).astype(np.float32)
    return incl_upper, strict_upper


def _dispatch_kernel(h_ref, wt_ref, wtok_ref, iu_ref, su_ref, o_ref):
    last = pl.num_programs(0) - 1

    @pl.when(pl.program_id(0) == last)
    def _():
        o_ref[...] = jnp.zeros_like(o_ref)

    @pl.when(pl.program_id(0) < last)
    def _():
        tm = h_ref.shape[0]
        r1 = o_ref.shape[0]
        sel = jnp.where(wt_ref[...] > 0.0, 1.0, 0.0)
        sel = jnp.concatenate([sel, jnp.zeros((EXPERT_LANES - N_EXPERTS, tm), F32)], axis=0)
        chosen = sel > 0.5
        mask = sel.astype(BF16)
        cnt = _dot(mask, iu_ref[...])
        ones_lanes = jnp.ones((tm, EXPERT_LANES), BF16)
        len_col = _ceil_seg(_dot(mask, ones_lanes))
        off_col = _dot_tn(su_ref[...], len_col.astype(BF16))
        len_row = _ceil_seg(_dot_nt(jnp.ones((8, tm), BF16), mask))
        off_row = _dot(len_row.astype(BF16), su_ref[...])
        dest = jnp.where(chosen, cnt + jnp.concatenate([off_col] * (tm // EXPERT_LANES), axis=1) - 1.0, -1.0)
        dhi, dlo = _split_256(dest)
        rows_e = lax.broadcasted_iota(jnp.int32, (r1, EXPERT_LANES), 0).astype(F32)
        lo = off_row[0:1, :]
        seg = jnp.where(rows_e >= lo, jnp.where(rows_e < lo + len_row[0:1, :], 1.0, 0.0), 0.0)
        segb = seg.astype(BF16)
        want = 256.0 * _dot(segb, dhi) + _dot(segb, dlo)
        rows_t = lax.broadcasted_iota(jnp.int32, (r1, tm), 0).astype(F32)
        onehot = jnp.where(want == rows_t, 1.0, 0.0).astype(BF16)
        xs = _dot(onehot, h_ref[...])
        wtok = wtok_ref[...]
        w_hi = wtok.astype(BF16)
        w_lo = (wtok - w_hi.astype(F32)).astype(BF16)
        w_row = jnp.sum((_dot(onehot, w_hi) + _dot(onehot, w_lo)) * seg, axis=1, keepdims=True)
        o_ref[:, 0:PACK_W] = _pack_pairs(xs)
        o_ref[:, PACK_W:ROW_W] = pltpu.bitcast(jnp.broadcast_to(w_row, (r1, HEAD_W)), jnp.int32)


def _dispatch(h2, w_t, w_tok):
    n, d = h2.shape
    tiles = n // ROW_TILE
    iu, su = _moe_tables(ROW_TILE)
    last = tiles - 1
    return pl.pallas_call(
        _dispatch_kernel,
        grid=(tiles + 1,),
        in_specs=[pl.BlockSpec((ROW_TILE, d), lambda i: (jnp.minimum(i, last), 0)),
                  pl.BlockSpec((N_EXPERTS, ROW_TILE), lambda i: (0, jnp.minimum(i, last))),
                  pl.BlockSpec((ROW_TILE, EXPERT_LANES), lambda i: (jnp.minimum(i, last), 0)),
                  pl.BlockSpec((ROW_TILE, ROW_TILE), lambda i: (0, 0)),
                  pl.BlockSpec((EXPERT_LANES, EXPERT_LANES), lambda i: (0, 0))],
        out_specs=pl.BlockSpec((TILE_ROWS, ROW_W), lambda i: (i, 0)),
        out_shape=jax.ShapeDtypeStruct(((tiles + 1) * TILE_ROWS, ROW_W), jnp.int32),
        compiler_params=_cparams(("parallel",)),
        name="moe_dispatch",
    )(h2, w_t, w_tok, jnp.asarray(iu, BF16), jnp.asarray(su, BF16))


def _moe_plan(w_t, tiles):
    cb = CHUNKS_PER_BLOCK
    cnt = jnp.sum((w_t > 0.0).reshape(N_EXPERTS, tiles, ROW_TILE), axis=-1, dtype=jnp.int32)
    nch = (cnt + SEG_ALIGN - 1) // SEG_ALIGN
    tile_chunk0 = jnp.cumsum(nch, axis=0) - nch
    per_expert = jnp.sum(nch, axis=1)
    padded = (per_expert + cb - 1) // cb * cb
    expert_end = jnp.cumsum(padded)
    expert_start = expert_end - padded
    seg_start = (expert_start[:, None] + jnp.cumsum(nch, axis=1) - nch).reshape(-1)
    seg_len = nch.reshape(-1)
    seg_src = (jnp.arange(tiles, dtype=jnp.int32)[None, :] * (TILE_ROWS // SEG_ALIGN) + tile_chunk0).reshape(-1)
    n_tok = tiles * ROW_TILE
    max_chunks = n_tok * TOP_K // SEG_ALIGN + tiles * N_EXPERTS + N_EXPERTS * (cb - 1)
    n_blocks = -(-max_chunks // cb)
    per_seg = jnp.stack([seg_start, seg_len, seg_src], axis=1)
    delta = per_seg - jnp.concatenate([jnp.zeros((1, 3), jnp.int32), per_seg[:-1]], axis=0)
    filled = jnp.cumsum(jnp.zeros((n_blocks * cb, 3), jnp.int32).at[seg_start].add(delta, mode='drop'), axis=0)
    c = jnp.arange(n_blocks * cb, dtype=jnp.int32) - filled[:, 0]
    used = c < filled[:, 1]
    spare0 = tiles * (TILE_ROWS // SEG_ALIGN)
    k = jnp.arange(n_blocks * cb, dtype=jnp.int32)
    read_src = jnp.where(used, filled[:, 2] + c, spare0 + 2 * cb).astype(jnp.int32)
    write_dst = jnp.where(used, filled[:, 2] + c, spare0 + k % (2 * cb)).astype(jnp.int32)
    blk = jnp.arange(n_blocks, dtype=jnp.int32) * cb
    blk_e = jnp.minimum(jnp.sum((expert_end[None, :] <= blk[:, None]).astype(jnp.int32), axis=1), N_EXPERTS - 1)
    n_active = (expert_end[-1] // cb).astype(jnp.int32).reshape(1)
    return blk_e, n_active, read_src, write_dst


def _expert_kernel(blk_e_ref, nact_ref, rsrc_ref, wdst_ref, xs_hbm, wg_ref, wu_ref, wd_ref, out_hbm,
                   xbuf, obuf, gsem, ssem):
    del blk_e_ref
    b = pl.program_id(0)
    n_active = nact_ref[0]
    cb = CHUNKS_PER_BLOCK

    def chunk_copies(blk, slot, gather, start):
        for k in range(cb):
            local = pl.ds(k * SEG_ALIGN, SEG_ALIGN)
            if gather:
                ch = rsrc_ref[blk * cb + k]
                rows = pl.ds(pl.multiple_of(ch * SEG_ALIGN, SEG_ALIGN), SEG_ALIGN)
                cp = pltpu.make_async_copy(xs_hbm.at[rows], xbuf.at[slot, local], gsem.at[slot])
            else:
                ch = wdst_ref[blk * cb + k]
                rows = pl.ds(pl.multiple_of(ch * SEG_ALIGN, SEG_ALIGN), SEG_ALIGN)
                cp = pltpu.make_async_copy(obuf.at[slot, local], out_hbm.at[rows], ssem.at[slot])
            if start:
                cp.start()
            else:
                cp.wait()

    @pl.when(b == 0)
    def _():
        chunk_copies(0, 0, True, True)

    @pl.when(b < n_active)
    def _():
        slot = b % 2

        @pl.when(b + 1 < n_active)
        def _():
            chunk_copies(b + 1, 1 - slot, True, True)

        chunk_copies(b, slot, True, False)

        @pl.when(b >= 2)
        def _():
            chunk_copies(b - 2, slot, False, False)

        x = xbuf[slot]
        xb = _unpack_pairs(x[:, 0:PACK_W])
        w = pltpu.bitcast(x[:, PACK_W:ROW_W], F32)
        hid = _silu(_dot(xb, wg_ref[0])) * _dot(xb, wu_ref[0]) * jnp.concatenate([w, w], axis=1)
        obuf[slot, :, 0:PACK_W] = _pack_pairs(_dot(hid.astype(BF16), wd_ref[0]))
        obuf[slot, :, PACK_W:ROW_W] = x[:, PACK_W:ROW_W]
        chunk_copies(b, slot, False, True)

        @pl.when(b == n_active - 1)
        def _():
            chunk_copies(b, slot, False, False)

            @pl.when(b >= 1)
            def _():
                chunk_copies(b - 1, 1 - slot, False, False)


def _experts(xs, plan, wg, wu, wd):
    blk_e, n_active, read_src, write_dst = plan
    n_blocks = blk_e.shape[0]
    d = wg.shape[1]
    wspec = lambda shape: pl.BlockSpec((1,) + shape, lambda b, be, na, rs, wd_: (be[b], 0, 0))
    return pl.pallas_call(
        _expert_kernel,
        grid_spec=pltpu.PrefetchScalarGridSpec(
            num_scalar_prefetch=4,
            grid=(n_blocks,),
            in_specs=[pl.BlockSpec(memory_space=pl.ANY),
                      wspec((d, EXPERT_DIM)), wspec((d, EXPERT_DIM)), wspec((EXPERT_DIM, d))],
            out_specs=pl.BlockSpec(memory_space=pl.ANY),
            scratch_shapes=[pltpu.VMEM((2, EXPERT_BLOCK, ROW_W), jnp.int32),
                            pltpu.VMEM((2, EXPERT_BLOCK, ROW_W), jnp.int32),
                            pltpu.SemaphoreType.DMA((2,)), pltpu.SemaphoreType.DMA((2,))]),
        out_shape=jax.ShapeDtypeStruct(xs.shape, xs.dtype),
        input_output_aliases={4: 0},
        compiler_params=_cparams(("arbitrary",)),
        name="moe_experts",
    )(blk_e, n_active, read_src, write_dst, xs, wg, wu, wd)


def _combine_kernel(ys_ref, wtok_ref, h_ref, x1_ref, mod_ref, ln_ref, il_ref, su_ref,
                    sg_ref, su2_ref, sd_ref, o_ref, *, alpha):
    tm = h_ref.shape[0]
    r1 = ys_ref.shape[0]
    chosen = wtok_ref[...] > 0.0
    mask = jnp.where(chosen, 1.0, 0.0).astype(BF16)
    cnt = _dot(il_ref[...], mask)
    len_row = _ceil_seg(_dot(jnp.ones((8, tm), BF16), mask))
    off_row = _dot(len_row.astype(BF16), su_ref[...])
    dhi, dlo = _split_256(jnp.where(chosen, cnt + off_row[0:1, :] - 1.0, -1.0))
    len_col = _ceil_seg(_dot_tn(mask, jnp.ones((tm, EXPERT_LANES), BF16)))
    off_col = _dot_tn(su_ref[...], len_col.astype(BF16))
    reps = r1 // EXPERT_LANES
    lo = jnp.concatenate([off_col] * reps, axis=1)
    hi = lo + jnp.concatenate([len_col] * reps, axis=1)
    rows_e = lax.broadcasted_iota(jnp.int32, (EXPERT_LANES, r1), 1).astype(F32)
    segb = jnp.where(rows_e >= lo, jnp.where(rows_e < hi, 1.0, 0.0), 0.0).astype(BF16)
    want = 256.0 * _dot(dhi, segb) + _dot(dlo, segb)
    rows_t = lax.broadcasted_iota(jnp.int32, (tm, r1), 1).astype(F32)
    onehot = jnp.where(want == rows_t, 1.0, 0.0).astype(BF16)
    y = _dot(onehot, _unpack_pairs(ys_ref[:, 0:PACK_W]))
    x = h_ref[...]
    y = y + _dot((_silu(_dot(x, sg_ref[...])) * _dot(x, su2_ref[...])).astype(BF16), sd_ref[...])
    gate2 = mod_ref[0, 0, 5:6, :]
    o_ref[...] = _ln(alpha * x1_ref[...] + gate2 * y) * ln_ref[0:1, :] + ln_ref[1:2, :]


def _combine(ys, w_tok, h2, x1, mods, ln2, sg, su, sd, alpha, tiles_per_seq, n_ctx_tiles):
    n, d = h2.shape
    tiles = n // ROW_TILE
    iu, sup = _moe_tables(ROW_TILE)
    full = lambda arr: pl.BlockSpec(arr.shape, lambda i: (0,) * arr.ndim)
    row = pl.BlockSpec((ROW_TILE, d), lambda i: (i, 0))
    il = jnp.asarray(iu.T, BF16)
    sup = jnp.asarray(sup, BF16)
    return pl.pallas_call(
        functools.partial(_combine_kernel, alpha=alpha),
        grid=(tiles,),
        in_specs=[pl.BlockSpec((TILE_ROWS, ROW_W), lambda i: (i, 0)),
                  pl.BlockSpec((ROW_TILE, EXPERT_LANES), lambda i: (i, 0)),
                  row, row,
                  pl.BlockSpec((1, 1, 6, d), lambda i: (i // tiles_per_seq,
                                                        (i % tiles_per_seq >= n_ctx_tiles).astype(jnp.int32), 0, 0)),
                  full(ln2), full(il), full(sup), full(sg), full(su), full(sd)],
        out_specs=row,
        out_shape=jax.ShapeDtypeStruct((n, d), F32),
        compiler_params=_cparams(("parallel",)),
        name="moe_combine",
    )(ys, w_tok, h2, x1, mods, ln2, il, sup, sg, su, sd)


def _rope_tables(n_lat, n_ctx):
    rows = n_lat // GRID_W
    row = jnp.repeat(jnp.arange(rows, dtype=F32), GRID_W)
    col = (jnp.arange(n_lat) % GRID_W).astype(F32)
    n_freq = DA_DH // 4
    inv = ROPE_THETA ** (-jnp.arange(n_freq, dtype=F32) / n_freq)
    ang = jnp.concatenate([row[:, None] * inv, col[:, None] * inv], axis=-1)
    ang = jnp.concatenate([ang, ang], axis=-1)
    cos, sin = jnp.cos(ang), jnp.sin(ang)
    sign = jnp.where(jnp.arange(DA_DH) < DA_DH // 2, -1.0, 1.0).astype(F32)
    cos = jnp.tile(cos, (1, HEAD_W // DA_DH))
    sin = jnp.tile(sin * sign, (1, HEAD_W // DA_DH))
    cos = jnp.concatenate([jnp.ones((n_ctx, HEAD_W), F32), cos], axis=0)
    sin = jnp.concatenate([jnp.zeros((n_ctx, HEAD_W), F32), sin], axis=0)
    return cos, sin


def kernel(x, c, ctx, c_ctx, w_ada, b_ada, w_in, lb_logits, hg_norm_g, lam_params, da_norm_g, w_branch_a, w_branch_b, w_branch_c, w_out, ln1_g, ln1_b, ln2_g, ln2_b, w_router, router_bias, w_exp_gate, w_exp_up, w_exp_down, w_sh_gate, w_sh_up, w_sh_down):
    bsz, n_lat, d = x.shape
    n_ctx = ctx.shape[1]
    depth = w_in.shape[0]
    t = n_ctx + n_lat
    assert n_ctx % ROW_TILE == 0 and n_lat % ROW_TILE == 0
    assert n_ctx % HG_CHUNK == 0 and n_lat % HG_CHUNK == 0 and n_lat % GRID_W == 0
    n_ctx_tiles = n_ctx // ROW_TILE
    alpha = (2 * depth) ** 0.25
    per = N_EXPERTS // N_GROUPS

    cos, sin = _rope_tables(n_lat, n_ctx)
    sm = jax.nn.softmax(lb_logits.astype(F32), axis=0)
    lower = jnp.cumsum(sm, axis=0) - sm[0:1]
    log_lb = jnp.log(jnp.maximum(lower, LB_FLOOR))
    log1m_lb = jnp.log1p(-lower)

    n_rows = -(-(bsz + 1) // 8) * 8
    cvec = jnp.concatenate([c, c_ctx[None, :], jnp.zeros((n_rows - bsz - 1, d), F32)], axis=0)

    n_branch_cols = w_in.shape[2] - 3 * d

    def member_major(a):
        return a.reshape((N_GROUPS, per) + a.shape[1:]).swapaxes(0, 1).reshape(a.shape)

    xs = jnp.concatenate([ctx, x], axis=1)
    for l in range(depth):
        mod = _ada(cvec, w_ada[l], b_ada[l])
        mod_lat = mod[:bsz].reshape(bsz, 1, 6, d)
        mod_ctx = jnp.broadcast_to(mod[bsz].reshape(1, 1, 6, d), (bsz, 1, 6, d))
        mods = jnp.concatenate([mod_ctx, mod_lat], axis=1)

        w_in_l = jnp.concatenate([w_in[l][:, n_branch_cols:], w_in[l][:, :n_branch_cols]], axis=1)
        p = _inproj(xs, mods, w_in_l.astype(BF16), n_ctx)
        lb_f = jnp.stack([log_lb[l, 0], log1m_lb[l, 0]])
        lb_b = jnp.stack([log_lb[l, 1], log1m_lb[l, 1]])
        o_f, o_b = _hgrn(p, lb_f, lb_b, n_ctx)

        lam_init = 0.8 - 0.6 * math.exp(-0.3 * l)
        lp = lam_params[l].astype(F32)
        lam = jnp.exp(jnp.sum(lp[0] * lp[1])) - jnp.exp(jnp.sum(lp[2] * lp[3])) + lam_init
        qr, kr, vr = _attn_prep(p, cos, sin)
        at = _attention(qr, kr, vr, jnp.full((8, HEAD_W), lam, F32), n_ctx)
        fr = _fourier(p, n_ctx)

        ln1 = jnp.stack([ln1_g[l], ln1_b[l]])
        hg_g = hg_norm_g[l].reshape(1, HEAD_W)
        da_g = da_norm_g[l].reshape(1, HEAD_W)
        x1, h2, logits_t = _merge(
            xs, mods, p, o_f, o_b, at, fr, hg_g, da_g,
            w_branch_a[l].astype(BF16), w_branch_b[l].astype(BF16), w_branch_c[l].astype(BF16),
            w_out[l].astype(BF16), ln1, member_major(w_router[l].T), alpha, 1.0 - lam_init, n_ctx_tiles)

        bias_col = jnp.broadcast_to(member_major(router_bias[l])[:, None], (N_EXPERTS, ROUTER_TILE))
        wts_t = _router(logits_t, bias_col)
        n_tok = bsz * t
        w_tok = jnp.concatenate([wts_t.T, jnp.zeros((n_tok, EXPERT_LANES - N_EXPERTS), F32)], axis=1)
        h2f = h2.reshape(n_tok, d)
        rows = _dispatch(h2f, wts_t, w_tok)
        plan = _moe_plan(wts_t, n_tok // ROW_TILE)
        rows = _experts(rows, plan, member_major(w_exp_gate[l]).astype(BF16),
                        member_major(w_exp_up[l]).astype(BF16), member_major(w_exp_down[l]).astype(BF16))
        ln2 = jnp.stack([ln2_g[l], ln2_b[l]])
        xs = _combine(rows, w_tok, h2f, x1.reshape(n_tok, d), mods, ln2,
                      w_sh_gate[l].astype(BF16), w_sh_up[l].astype(BF16), w_sh_down[l].astype(BF16),
                      alpha, t // ROW_TILE, n_ctx_tiles).reshape(bsz, t, d)
    return xs[:, n_ctx:]
```

```python
import functools
import math

import numpy as np
import jax
import jax.numpy as jnp
from jax import lax
from jax.experimental import pallas as pl
from jax.experimental.pallas import tpu as pltpu

F32 = jnp.float32
BF16 = jnp.bfloat16

HG_HEADS = 4
HG_DK = 128
LB_FLOOR = 1e-30
DA_HEADS = 4
DA_DH = 64
ROPE_THETA = 10000.0
GRID_W = 64
FR_GROUPS = 4
FR_DG = 128
N_EXPERTS = 64
TOP_K = 8
EXPERT_DIM = 256
N_GROUPS = 8
TOPK_GROUPS = 4
ROUTED_SCALE = 2.5
MASK_VALUE = -1e9
LN_EPS = 1e-5
LOG2_E = 1.4426950408889634

HEAD_W = 128
BRANCH_W = 512
GATES_BLOCKS = 6
COL_Q, COL_FF, COL_FB, COL_I, COL_OG, COL_QA, COL_KA, COL_VA, COL_U = range(6, 15)

ROW_TILE = 256
HG_CHUNK = 128
HG_LEVELS = (64, 32, 16, 8, 4, 2, 1)
HG_ROW_REF_MIN = 8
IN_N_TILE = 2560
IN_ROW_TILE = 768
ROUTER_TILE = 512
VMEM_LIMIT = 48 * 1024 * 1024


def _cparams(sem):
    return pltpu.CompilerParams(dimension_semantics=sem, vmem_limit_bytes=VMEM_LIMIT)


def _dot(a, b):
    return jnp.dot(a, b, preferred_element_type=F32)


def _dot_nt(a, b):
    return lax.dot_general(a, b, (((1,), (1,)), ((), ())), preferred_element_type=F32)


def _dot_tn(a, b):
    return lax.dot_general(a, b, (((0,), (0,)), ((), ())), preferred_element_type=F32)


def _silu(x):
    return x * jax.nn.sigmoid(x)


def _ln(x):
    mu = jnp.mean(x, axis=-1, keepdims=True)
    xc = x - mu
    var = jnp.mean(xc * xc, axis=-1, keepdims=True)
    return xc * lax.rsqrt(var + LN_EPS)


def _ada_kernel(c_ref, w_ref, b_ref, o_ref):
    o_ref[...] = jnp.dot(_silu(c_ref[...]), w_ref[...], preferred_element_type=F32,
                         precision=lax.Precision.HIGHEST) + b_ref[...]


def _ada(cvec, w, b):
    rows, d = cvec.shape
    n = w.shape[1]
    tn = 1536
    return pl.pallas_call(
        _ada_kernel,
        grid=(n // tn,),
        in_specs=[pl.BlockSpec((rows, d), lambda j: (0, 0)),
                  pl.BlockSpec((d, tn), lambda j: (0, j)),
                  pl.BlockSpec((1, tn), lambda j: (0, j))],
        out_specs=pl.BlockSpec((rows, tn), lambda j: (0, j)),
        out_shape=jax.ShapeDtypeStruct((rows, n), F32),
        compiler_params=_cparams(("parallel",)),
        name="ada",
    )(cvec, w, b.reshape(1, n))


def _inproj_kernel(x_ref, mod_ref, w_ref, o_ref, *, n_ctx):
    first = pl.program_id(2) == 0
    xn = _ln(x_ref[0])
    shift_top = jnp.where(first, mod_ref[0, 0, 0:1, :], mod_ref[0, 1, 0:1, :])
    scale_top = jnp.where(first, mod_ref[0, 0, 1:2, :], mod_ref[0, 1, 1:2, :])
    parts = [xn[:n_ctx] * (1.0 + scale_top) + shift_top]
    if xn.shape[0] > n_ctx:
        parts.append(xn[n_ctx:] * (1.0 + mod_ref[0, 1, 1:2, :]) + mod_ref[0, 1, 0:1, :])
    h = jnp.concatenate(parts, axis=0) if len(parts) > 1 else parts[0]
    o_ref[0] = _dot(h.astype(BF16), w_ref[...])


def _inproj(x, mods, w, n_ctx):
    bsz, t, d = x.shape
    n = w.shape[1]
    tm = IN_ROW_TILE if t % IN_ROW_TILE == 0 and n_ctx <= IN_ROW_TILE else n_ctx
    return pl.pallas_call(
        functools.partial(_inproj_kernel, n_ctx=n_ctx),
        grid=(n // IN_N_TILE, bsz, t // tm),
        in_specs=[pl.BlockSpec((1, tm, d), lambda j, b, i: (b, i, 0)),
                  pl.BlockSpec((1, 2, 6, d), lambda j, b, i: (b, 0, 0, 0)),
                  pl.BlockSpec((d, IN_N_TILE), lambda j, b, i: (0, j))],
        out_specs=pl.BlockSpec((1, tm, IN_N_TILE), lambda j, b, i: (b, i, j)),
        out_shape=jax.ShapeDtypeStruct((bsz, t, n), F32),
        compiler_params=_cparams(("parallel", "parallel", "parallel")),
        name="inproj",
    )(x, mods, w)


def _hgrn_tables(reverse):
    c = HG_CHUNK
    t = np.arange(c)
    if reverse:
        cum = (t[None, :] >= t[:, None])
    else:
        cum = (t[None, :] <= t[:, None])
    mats = [cum]
    for m in HG_LEVELS:
        if m < HG_ROW_REF_MIN:
            idx = (t // (2 * m)) * (2 * m) + m - (1 if reverse else 0)
            mats.append(cum[idx])
    stack = np.concatenate(mats, axis=0).astype(np.float32)
    x = t[:, None] ^ t[None, :]
    lvl = np.zeros((c, c), np.int32)
    for li, m in enumerate(HG_LEVELS):
        lvl = np.where((x >= m) & (x < 2 * m), li + 1, lvl)
    visible = (t[None, :] >= t[:, None]) if reverse else (t[None, :] <= t[:, None])
    lvl = np.where(visible, lvl, -1).astype(np.int32)
    return stack, lvl


def _hgrn_kernel(qf_ref, ff_ref, vf_ref, qb_ref, fb_ref, vb_ref, lbf_ref, lbb_ref,
                 stf_ref, lvf_ref, stb_ref, lvb_ref, of_ref, ob_ref, sf_ref, sb_ref):
    c = HG_CHUNK

    @pl.when(pl.program_id(1) == 0)
    def _():
        sf_ref[...] = jnp.zeros_like(sf_ref)
        sb_ref[...] = jnp.zeros_like(sb_ref)

    chains = []
    for h in range(HG_HEADS):
        chains.append((qf_ref, ff_ref, vf_ref, lbf_ref, stf_ref, lvf_ref, of_ref, sf_ref, h, False))
        chains.append((qb_ref, fb_ref, vb_ref, lbb_ref, stb_ref, lvb_ref, ob_ref, sb_ref, h, True))

    gates = []
    for q_ref, f_ref, v_ref, lb_ref, _, _, _, _, h, _ in chains:
        sl = slice(h * HEAD_W, (h + 1) * HEAD_W)
        q = _silu(q_ref[0, :, sl])
        fr = f_ref[0, :, sl]
        v = v_ref[0, :, sl].astype(BF16)
        log_lb = lb_ref[0:1, sl]
        log1m_lb = lb_ref[1:2, sl]
        c2 = log1m_lb + jnp.minimum(fr, 0.0) - jnp.log1p(jnp.exp(-jnp.abs(fr)))
        g = jnp.maximum(log_lb, c2) + jnp.log1p(jnp.exp(-jnp.abs(log_lb - c2)))
        gates.append((q, 1.0 - jnp.exp(g), v, g))

    cums = []
    for (_, _, _, _, stack_ref, _, _, _, _, _), (_, _, _, g) in zip(chains, gates):
        g1 = g.astype(BF16)
        r1 = g - g1.astype(F32)
        g2 = r1.astype(BF16)
        g3 = (r1 - g2.astype(F32)).astype(BF16)
        stack = stack_ref[...]
        cums.append(_dot(stack, g1) + _dot(stack, g2) + _dot(stack, g3))

    acc = []
    qk16 = [(q.astype(BF16), k.astype(BF16)) for q, k, _, _ in gates]
    for (_, _, _, _, _, lvl_ref, _, _, _, _), (q16, k16) in zip(chains, qk16):
        acc.append(jnp.where(lvl_ref[...] == 0, _dot_nt(q16, k16), 0.0))
    n_stacked = 0
    for li, m in enumerate(HG_LEVELS):
        if m < HG_ROW_REF_MIN:
            n_stacked += 1
        for ci, (chain, (q16, k16), bb) in enumerate(zip(chains, qk16, cums)):
            reverse = chain[9]
            b = bb[0:c]
            if m >= HG_ROW_REF_MIN:
                rows = []
                for j in range(c // (2 * m)):
                    idx = j * 2 * m + m - (1 if reverse else 0)
                    rows.append(jnp.broadcast_to(b[idx:idx + 1, :], (2 * m, HEAD_W)))
                ref = rows[0] if len(rows) == 1 else jnp.concatenate(rows, axis=0)
            else:
                ref = bb[n_stacked * c:(n_stacked + 1) * c]
            e = jnp.exp(-jnp.abs(b - ref)).astype(BF16)
            acc[ci] = jnp.where(chain[5][...] == li + 1, _dot_nt(q16 * e, k16 * e), acc[ci])

    for chain, (q, k, v, _), bb, a in zip(chains, gates, cums, acc):
        o_ref, st_ref, h, reverse = chain[6], chain[7], chain[8], chain[9]
        sl = slice(h * HEAD_W, (h + 1) * HEAD_W)
        b = bb[0:c]
        st = st_ref[h]
        o_ref[0, :, sl] = _dot(a.astype(BF16), v) + _dot_nt((q * jnp.exp(b)).astype(BF16), st.astype(BF16))
        b_end = b[0:1] if reverse else b[c - 1:c]
        k_end = (k * jnp.exp(b_end - b)).astype(BF16)
        st_ref[h] = st * jnp.exp(b_end) + _dot_tn(v, k_end)


def _hgrn(p, lb_f, lb_b, n_ctx):
    bsz, t, _ = p.shape
    c = HG_CHUNK
    nc, nt = n_ctx // c, t // c
    stack_f, lvl_f = _hgrn_tables(False)
    stack_b, lvl_b = _hgrn_tables(True)

    def rev(j):
        return jnp.where(j < nc, nc - 1 - j, nc + nt - 1 - j)

    nst = stack_f.shape[0]
    blk = lambda row, col: pl.BlockSpec((1, c, BRANCH_W), lambda b, j: (b, row(j), col))
    const = lambda shape: pl.BlockSpec(shape, lambda b, j: (0, 0))
    fwd = lambda j: j
    out = jax.ShapeDtypeStruct((bsz, t, BRANCH_W), F32)
    state = pltpu.VMEM((HG_HEADS, HEAD_W, HEAD_W), F32)
    return pl.pallas_call(
        _hgrn_kernel,
        grid=(bsz, nt),
        in_specs=[blk(fwd, COL_Q), blk(fwd, COL_FF), blk(fwd, COL_I),
                  blk(rev, COL_Q), blk(rev, COL_FB), blk(rev, COL_I),
                  const((2, BRANCH_W)), const((2, BRANCH_W)),
                  const((nst, c)), const((c, c)), const((nst, c)), const((c, c))],
        out_specs=[blk(fwd, 0), blk(rev, 0)],
        out_shape=[out, out],
        scratch_shapes=[state, state],
        compiler_params=_cparams(("parallel", "arbitrary")),
        name="hgrn",
    )(p, p, p, p, p, p, lb_f, lb_b, jnp.asarray(stack_f, BF16), jnp.asarray(lvl_f),
      jnp.asarray(stack_b, BF16), jnp.asarray(lvl_b))


def _attn_prep_kernel(q_ref, k_ref, v_ref, cos_ref, sin_ref, qo_ref, ko_ref, vo_ref):
    cos = cos_ref[...]
    sin = sin_ref[...]
    lane = lax.broadcasted_iota(jnp.int32, cos.shape, 1)
    first_half = (lane % DA_DH) < (DA_DH // 2)

    def rope(x):
        up = pltpu.roll(x, HEAD_W - DA_DH // 2, 1)
        down = pltpu.roll(x, DA_DH // 2, 1)
        return x * cos + jnp.where(first_half, up, down) * sin

    ones = jnp.ones((q_ref.shape[1], HEAD_W), BF16)
    for h in range(DA_HEADS):
        sl = slice(h * HEAD_W, (h + 1) * HEAD_W)
        qo_ref[0, :, sl] = (rope(q_ref[0, :, sl]) * (DA_DH ** -0.5 * LOG2_E)).astype(BF16)
        ko_ref[0, :, sl] = rope(k_ref[0, :, sl]).astype(BF16)
        vo_ref[0, :, 2 * h * HEAD_W:(2 * h + 1) * HEAD_W] = v_ref[0, :, sl].astype(BF16)
        vo_ref[0, :, (2 * h + 1) * HEAD_W:(2 * h + 2) * HEAD_W] = ones


def _attn_prep(p, cos, sin):
    bsz, t, _ = p.shape
    spec = lambda col: pl.BlockSpec((1, ROW_TILE, BRANCH_W), lambda b, i: (b, i, col))
    tab = pl.BlockSpec((ROW_TILE, HEAD_W), lambda b, i: (i, 0))
    out = jax.ShapeDtypeStruct((bsz, t, BRANCH_W), BF16)
    return pl.pallas_call(
        _attn_prep_kernel,
        grid=(bsz, t // ROW_TILE),
        in_specs=[spec(COL_QA), spec(COL_KA), spec(COL_VA), tab, tab],
        out_specs=[spec(0), spec(0), pl.BlockSpec((1, ROW_TILE, 2 * BRANCH_W), lambda b, i: (b, i, 0))],
        out_shape=[out, out, jax.ShapeDtypeStruct((bsz, t, 2 * BRANCH_W), BF16)],
        compiler_params=_cparams(("parallel", "parallel")),
        name="attn_prep",
    )(p, p, p, cos, sin)


def _attn_kernel(q_ref, k_ref, v_ref, lam_ref, o_ref, *, n_ctx, n_ctx_tiles):
    lam = lam_ref[0:1, 0:1]

    def attend(n_keys):
        k = k_ref[0, 0:n_keys, :]
        v = v_ref[0, 0:n_keys, :]
        half = q_ref.shape[1] // 2
        lane = lax.broadcasted_iota(jnp.int32, (half, HEAD_W), 1)
        chains = [(r, first) for r in range(2) for first in (True, False)]
        scores = []
        for r, first in chains:
            q = q_ref[0, r * half:(r + 1) * half, :]
            qm = jnp.where((lane < DA_DH) == first, q, jnp.zeros_like(q))
            scores.append(_dot_nt(qm, k))
        probs = [jnp.exp2(s - jnp.max(s, axis=-1, keepdims=True)).astype(BF16) for s in scores]
        outs = []
        for e in probs:
            pv = _dot(e, v)
            outs.append(pv[:, 0:HEAD_W] / pv[:, HEAD_W:2 * HEAD_W])
        for r in range(2):
            o_ref[0, r * half:(r + 1) * half, :] = outs[2 * r] - lam * outs[2 * r + 1]

    is_ctx = pl.program_id(2) < n_ctx_tiles

    @pl.when(is_ctx)
    def _():
        attend(n_ctx)

    @pl.when(jnp.logical_not(is_ctx))
    def _():
        attend(k_ref.shape[1])


def _attention(q, k, v, lam, n_ctx):
    bsz, t, _ = q.shape
    n_ctx_tiles = n_ctx // ROW_TILE
    return pl.pallas_call(
        functools.partial(_attn_kernel, n_ctx=n_ctx, n_ctx_tiles=n_ctx_tiles),
        grid=(bsz, DA_HEADS, t // ROW_TILE),
        in_specs=[pl.BlockSpec((1, ROW_TILE, HEAD_W), lambda b, h, i: (b, i, h)),
                  pl.BlockSpec((1, t, HEAD_W), lambda b, h, i: (b, 0, h)),
                  pl.BlockSpec((1, t, 2 * HEAD_W), lambda b, h, i: (b, 0, h)),
                  pl.BlockSpec((8, HEAD_W), lambda b, h, i: (0, 0))],
        out_specs=pl.BlockSpec((1, ROW_TILE, HEAD_W), lambda b, h, i: (b, i, h)),
        out_shape=jax.ShapeDtypeStruct((bsz, t, BRANCH_W), F32),
        compiler_params=_cparams(("parallel", "parallel", "parallel")),
        name="diff_attn",
    )(q, k, v, lam)


def _dft_tables(n):
    idx = np.arange(n, dtype=np.int64)
    ang = 2.0 * np.pi * ((idx[:, None] * idx[None, :]) % n).astype(np.float64) / n
    scale = 1.0 / math.sqrt(n * FR_DG)
    return np.concatenate([np.cos(ang), -np.sin(ang)], axis=1).astype(np.float32) * np.float32(scale)


def _channel_dft_tables():
    idx = np.arange(FR_DG, dtype=np.int64)
    ang = 2.0 * np.pi * ((idx[:, None] * idx[None, :]) % FR_DG).astype(np.float64) / FR_DG
    eye = np.eye(FR_GROUPS)
    return (np.kron(eye, np.cos(ang)).astype(np.float32), np.kron(eye, np.sin(ang)).astype(np.float32))


def _fourier_kernel(u_ref, cc_ref, sc_ref, wc_ref, wl_ref, o_ref, zc_ref, zl_ref, *, n_ctx, n_ctx_tiles):
    i = pl.program_id(1)
    n_lat = u_ref.shape[1] - n_ctx

    @pl.when(i == 0)
    def _():
        u = u_ref[0].astype(BF16)
        zcos = _dot(u, cc_ref[...]).astype(BF16)
        zsin = _dot(u, sc_ref[...]).astype(BF16)
        zc_ref[0:n_ctx] = zcos[0:n_ctx]
        zc_ref[n_ctx:2 * n_ctx] = zsin[0:n_ctx]
        zl_ref[0:n_lat] = zcos[n_ctx:]
        zl_ref[n_lat:2 * n_lat] = zsin[n_ctx:]

    @pl.when(i < n_ctx_tiles)
    def _():
        o_ref[0] = _dot(wc_ref[...], zc_ref[...])

    @pl.when(i >= n_ctx_tiles)
    def _():
        o_ref[0] = _dot(wl_ref[...], zl_ref[...])


def _fourier(p, n_ctx):
    bsz, t, _ = p.shape
    n_lat = t - n_ctx
    n_ctx_tiles = n_ctx // ROW_TILE
    cc, sc = _channel_dft_tables()
    wc = jnp.asarray(_dft_tables(n_ctx)).astype(BF16)
    wl = jnp.asarray(_dft_tables(n_lat)).astype(BF16)
    cc = jnp.asarray(cc).astype(BF16)
    sc = jnp.asarray(sc).astype(BF16)
    return pl.pallas_call(
        functools.partial(_fourier_kernel, n_ctx=n_ctx, n_ctx_tiles=n_ctx_tiles),
        grid=(bsz, t // ROW_TILE),
        in_specs=[pl.BlockSpec((1, t, BRANCH_W), lambda b, i: (b, 0, COL_U)),
                  pl.BlockSpec((BRANCH_W, BRANCH_W), lambda b, i: (0, 0)),
                  pl.BlockSpec((BRANCH_W, BRANCH_W), lambda b, i: (0, 0)),
                  pl.BlockSpec((ROW_TILE, 2 * n_ctx), lambda b, i: (jnp.minimum(i, n_ctx_tiles - 1), 0)),
                  pl.BlockSpec((ROW_TILE, 2 * n_lat), lambda b, i: (jnp.maximum(i - n_ctx_tiles, 0), 0))],
        out_specs=pl.BlockSpec((1, ROW_TILE, BRANCH_W), lambda b, i: (b, i, 0)),
        out_shape=jax.ShapeDtypeStruct((bsz, t, BRANCH_W), F32),
        scratch_shapes=[pltpu.VMEM((2 * n_ctx, BRANCH_W), BF16), pltpu.VMEM((2 * n_lat, BRANCH_W), BF16)],
        compiler_params=_cparams(("parallel", "arbitrary")),
        name="fourier",
    )(p, cc, sc, wc, wl)


def _head_rms(x, gain):
    parts = []
    for h in range(BRANCH_W // HEAD_W):
        xh = x[:, h * HEAD_W:(h + 1) * HEAD_W]
        parts.append(xh * lax.rsqrt(jnp.mean(xh * xh, axis=-1, keepdims=True) + LN_EPS) * gain)
    return jnp.concatenate(parts, axis=-1)


def _merge_kernel(x_ref, mod_ref, gates_ref, og_ref, of_ref, ob_ref, at_ref, fr_ref,
                  hg_ref, da_ref, wa_ref, wb_ref, wc_ref, wo_ref, ln_ref, wr_ref,
                  x1_ref, h2_ref, lg_ref, *, alpha, attn_scale):
    d = x_ref.shape[2]
    tm = x_ref.shape[1]
    gate1 = mod_ref[0, 0, 2:3, :]
    shift2 = mod_ref[0, 0, 3:4, :]
    scale2 = mod_ref[0, 0, 4:5, :]
    halves = [slice(r * (tm // 2), (r + 1) * (tm // 2)) for r in range(2)]
    a = [(_head_rms(of_ref[0, rs, :] + ob_ref[0, rs, :], hg_ref[...]) * _silu(og_ref[0, rs, :])).astype(BF16)
         for rs in halves]
    bq = [(_head_rms(at_ref[0, rs, :], da_ref[...]) * attn_scale).astype(BF16) for rs in halves]
    pa = [_dot(v, wa_ref[...]) for v in a]
    pb = [_dot(v, wb_ref[...]) for v in bq]
    pc = [_dot(fr_ref[0, rs, :].astype(BF16), wc_ref[...]) for rs in halves]
    m = []
    for rs, ya, yb, yc in zip(halves, pa, pb, pc):
        m.append((jax.nn.sigmoid(gates_ref[0, rs, 0:d]) * ya + jax.nn.sigmoid(gates_ref[0, rs, d:2 * d]) * yb
                  + jax.nn.sigmoid(gates_ref[0, rs, 2 * d:3 * d]) * yc).astype(BF16))
    mix = [_dot(v, wo_ref[...]) for v in m]
    h2s = []
    for rs, mx in zip(halves, mix):
        x1 = _ln(alpha * x_ref[0, rs, :] + gate1 * mx) * ln_ref[0:1, :] + ln_ref[1:2, :]
        x1_ref[0, rs, :] = x1
        h2 = _ln(x1) * (1.0 + scale2) + shift2
        h2_ref[0, rs, :] = h2.astype(BF16)
        h2s.append(h2)
    for rs, h2 in zip(halves, h2s):
        lg_ref[:, rs] = lax.dot_general(wr_ref[...], h2, (((1,), (1,)), ((), ())),
                                        preferred_element_type=F32, precision=lax.Precision.HIGHEST)


def _merge(x, mods, p, o_f, o_b, at, fr, hg_g, da_g, wa, wb, wc, wo, ln1, wr_t, alpha, attn_scale,
           n_ctx_tiles, skip_tiles):
    bsz, t, d = x.shape
    tiles = t // ROW_TILE - skip_tiles
    t_out = tiles * ROW_TILE
    row = lambda w, col: pl.BlockSpec((1, ROW_TILE, w), lambda b, i: (b, i + skip_tiles, col))
    out_row = pl.BlockSpec((1, ROW_TILE, d), lambda b, i: (b, i, 0))
    full = lambda arr: pl.BlockSpec(arr.shape, lambda b, i: (0,) * arr.ndim)
    return pl.pallas_call(
        functools.partial(_merge_kernel, alpha=alpha, attn_scale=attn_scale),
        grid=(bsz, tiles),
        in_specs=[row(d, 0),
                  pl.BlockSpec((1, 1, 6, d),
                               lambda b, i: (b, (i + skip_tiles >= n_ctx_tiles).astype(jnp.int32), 0, 0)),
                  row(GATES_BLOCKS * BRANCH_W, 0), row(BRANCH_W, COL_OG),
                  row(BRANCH_W, 0), row(BRANCH_W, 0), row(BRANCH_W, 0), row(BRANCH_W, 0),
                  full(hg_g), full(da_g), full(wa), full(wb), full(wc), full(wo), full(ln1), full(wr_t)],
        out_specs=[out_row, out_row,
                   pl.BlockSpec((N_EXPERTS, ROW_TILE), lambda b, i: (0, b * tiles + i))],
        out_shape=[jax.ShapeDtypeStruct((bsz, t_out, d), F32),
                   jax.ShapeDtypeStruct((bsz, t_out, d), BF16),
                   jax.ShapeDtypeStruct((N_EXPERTS, bsz * t_out), F32)],
        compiler_params=_cparams(("parallel", "parallel")),
        name="merge",
    )(x, mods, p, p, o_f, o_b, at, fr, hg_g, da_g, wa, wb, wc, wo, ln1, wr_t)


def _router_kernel(lg_ref, bias_ref, w_ref):
    per = N_EXPERTS // N_GROUPS
    scores = [jax.nn.sigmoid(lg_ref[j * N_GROUPS:(j + 1) * N_GROUPS, :]) for j in range(per)]
    sel = [scores[j] + bias_ref[j * N_GROUPS:(j + 1) * N_GROUPS, :] for j in range(per)]
    shape = sel[0].shape
    grp = lax.broadcasted_iota(jnp.int32, shape, 0)

    def beats(other, me, tie):
        if isinstance(tie, bool):
            return jnp.where((other >= me) if tie else (other > me), 1, 0)
        return jnp.where(other > me, 1, jnp.where(other == me, tie, 0))

    m1 = jnp.maximum(sel[0], sel[1])
    m2 = jnp.minimum(sel[0], sel[1])
    for j in range(2, per):
        m2 = jnp.maximum(m2, jnp.minimum(m1, sel[j]))
        m1 = jnp.maximum(m1, sel[j])
    gscore = m1 + m2
    grank = jnp.zeros(shape, jnp.int32)
    lower_group = [None] + [jnp.where(grp >= kk, 1, 0) for kk in range(1, N_GROUPS)]
    for kk in range(1, N_GROUPS):
        other = pltpu.roll(gscore, kk, 0)
        grank = grank + beats(other, gscore, lower_group[kk])
    gmask = grank < TOPK_GROUPS
    sel = [jnp.where(gmask, s, MASK_VALUE) for s in sel]
    rolled = [[None] + [pltpu.roll(s, kk, 0) for kk in range(1, N_GROUPS)] for s in sel]
    weights = []
    for j in range(per):
        rank = jnp.zeros(shape, jnp.int32)
        for j2 in range(per):
            for kk in range(N_GROUPS):
                if kk == 0 and j2 == j:
                    continue
                if kk == 0:
                    rank = rank + beats(sel[j2], sel[j], j2 < j)
                else:
                    rank = rank + beats(rolled[j2][kk], sel[j], lower_group[kk])
        weights.append(jnp.where(rank < TOP_K, scores[j], 0.0))
    total = weights[0]
    for j in range(1, per):
        total = total + weights[j]
    total = jnp.sum(total, axis=0, keepdims=True)
    for j in range(per):
        w_ref[j * N_GROUPS:(j + 1) * N_GROUPS, :] = weights[j] / total * ROUTED_SCALE


def _router(logits_t, bias_col):
    e, n = logits_t.shape
    return pl.pallas_call(
        _router_kernel,
        grid=(n // ROUTER_TILE,),
        in_specs=[pl.BlockSpec((e, ROUTER_TILE), lambda i: (0, i)),
                  pl.BlockSpec((e, ROUTER_TILE), lambda i: (0, 0))],
        out_specs=pl.BlockSpec((e, ROUTER_TILE), lambda i: (0, i)),
        out_shape=jax.ShapeDtypeStruct((e, n), F32),
        compiler_params=_cparams(("parallel",)),
        name="router",
    )(logits_t, bias_col)


SEG_ALIGN = 8
TILE_ROWS = ROW_TILE * TOP_K + N_EXPERTS * SEG_ALIGN
CHUNKS_PER_BLOCK = 64
EXPERT_BLOCK = CHUNKS_PER_BLOCK * SEG_ALIGN
PACK_W = 512
ROW_W = PACK_W + HEAD_W
EXPERT_LANES = 128
HI_MASK = -65536


def _pack_pairs(x):
    hi = pltpu.bitcast(x[:, :PACK_W].astype(jnp.bfloat16).astype(F32), jnp.int32)
    lo = pltpu.bitcast(x[:, PACK_W:].astype(jnp.bfloat16).astype(F32), jnp.int32)
    return hi | lax.shift_right_logical(lo, jnp.full_like(lo, 16))


def _pack_bf16_valued_pairs(x):
    hi = pltpu.bitcast(x[:, :PACK_W], jnp.int32) & HI_MASK
    lo = pltpu.bitcast(x[:, PACK_W:], jnp.int32)
    return hi | lax.shift_right_logical(lo, jnp.full_like(lo, 16))


def _unpack_pairs(u):
    hi = pltpu.bitcast(u & HI_MASK, F32)
    lo = pltpu.bitcast(lax.shift_left(u, jnp.full_like(u, 16)), F32)
    return jnp.concatenate([hi, lo], axis=1).astype(BF16)


def _ceil_seg(c):
    return jnp.floor((c + (SEG_ALIGN - 1.0)) * (1.0 / SEG_ALIGN)) * SEG_ALIGN


def _split_256(c):
    hi = jnp.floor(c * (1.0 / 256.0))
    return hi.astype(BF16), (c - 256.0 * hi).astype(BF16)


def _moe_tables(tm):
    t = np.arange(tm)
    e = np.arange(EXPERT_LANES)
    incl_upper = (t[:, None] <= t[None, :]).astype(np.float32)
    strict_upper = (e[:, None] < e[None, :]).astype(np.float32)
    return incl_upper, strict_upper


def _dispatch_kernel(h_ref, wt_ref, wtok_ref, iu_ref, su_ref, o_ref):
    last = pl.num_programs(0) - 1

    @pl.when(pl.program_id(0) == last)
    def _():
        o_ref[...] = jnp.zeros_like(o_ref)

    @pl.when(pl.program_id(0) < last)
    def _():
        tm = h_ref.shape[0]
        r1 = o_ref.shape[0]
        sel = jnp.where(wt_ref[...] > 0.0, 1.0, 0.0)
        sel = jnp.concatenate([sel, jnp.zeros((EXPERT_LANES - N_EXPERTS, tm), F32)], axis=0)
        chosen = sel > 0.5
        mask = sel.astype(BF16)
        cnt = _dot(mask, iu_ref[...])
        ones_lanes = jnp.ones((tm, EXPERT_LANES), BF16)
        len_col = _ceil_seg(_dot(mask, ones_lanes))
        off_col = _dot_tn(su_ref[...], len_col.astype(BF16))
        len_row = _ceil_seg(_dot_nt(jnp.ones((8, tm), BF16), mask))
        off_row = _dot(len_row.astype(BF16), su_ref[...])
        dest = jnp.where(chosen, cnt + jnp.concatenate([off_col] * (tm // EXPERT_LANES), axis=1) - 1.0, -1.0)
        dhi, dlo = _split_256(dest)
        rows_e = lax.broadcasted_iota(jnp.int32, (r1, EXPERT_LANES), 0).astype(F32)
        lo = off_row[0:1, :]
        seg = jnp.where(rows_e >= lo, jnp.where(rows_e < lo + len_row[0:1, :], 1.0, 0.0), 0.0)
        segb = seg.astype(BF16)
        want = 256.0 * _dot(segb, dhi) + _dot(segb, dlo)
        rows_t = lax.broadcasted_iota(jnp.int32, (r1, tm), 0).astype(F32)
        onehot = jnp.where(want == rows_t, 1.0, 0.0).astype(BF16)
        xs = _dot(onehot, h_ref[...])
        wtok = wtok_ref[...]
        w_hi = wtok.astype(BF16)
        w_lo = (wtok - w_hi.astype(F32)).astype(BF16)
        w_row = jnp.sum((_dot(onehot, w_hi) + _dot(onehot, w_lo)) * seg, axis=1, keepdims=True)
        o_ref[:, 0:PACK_W] = _pack_bf16_valued_pairs(xs)
        o_ref[:, PACK_W:ROW_W] = pltpu.bitcast(jnp.broadcast_to(w_row, (r1, HEAD_W)), jnp.int32)


def _dispatch(h2, w_t, w_tok):
    n, d = h2.shape
    tiles = n // ROW_TILE
    iu, su = _moe_tables(ROW_TILE)
    last = tiles - 1
    return pl.pallas_call(
        _dispatch_kernel,
        grid=(tiles + 1,),
        in_specs=[pl.BlockSpec((ROW_TILE, d), lambda i: (jnp.minimum(i, last), 0)),
                  pl.BlockSpec((N_EXPERTS, ROW_TILE), lambda i: (0, jnp.minimum(i, last))),
                  pl.BlockSpec((ROW_TILE, EXPERT_LANES), lambda i: (jnp.minimum(i, last), 0)),
                  pl.BlockSpec((ROW_TILE, ROW_TILE), lambda i: (0, 0)),
                  pl.BlockSpec((EXPERT_LANES, EXPERT_LANES), lambda i: (0, 0))],
        out_specs=pl.BlockSpec((TILE_ROWS, ROW_W), lambda i: (i, 0)),
        out_shape=jax.ShapeDtypeStruct(((tiles + 1) * TILE_ROWS, ROW_W), jnp.int32),
        compiler_params=_cparams(("parallel",)),
        name="moe_dispatch",
    )(h2, w_t, w_tok, jnp.asarray(iu, BF16), jnp.asarray(su, BF16))


def _moe_plan(w_t, tiles):
    cb = CHUNKS_PER_BLOCK
    cnt = jnp.sum((w_t > 0.0).reshape(N_EXPERTS, tiles, ROW_TILE), axis=-1, dtype=jnp.int32)
    nch = (cnt + SEG_ALIGN - 1) // SEG_ALIGN
    tile_chunk0 = jnp.cumsum(nch, axis=0) - nch
    per_expert = jnp.sum(nch, axis=1)
    padded = (per_expert + cb - 1) // cb * cb
    expert_end = jnp.cumsum(padded)
    expert_start = expert_end - padded
    seg_start = (expert_start[:, None] + jnp.cumsum(nch, axis=1) - nch).reshape(-1)
    seg_len = nch.reshape(-1)
    seg_src = (jnp.arange(tiles, dtype=jnp.int32)[None, :] * (TILE_ROWS // SEG_ALIGN) + tile_chunk0).reshape(-1)
    n_tok = tiles * ROW_TILE
    max_chunks = n_tok * TOP_K // SEG_ALIGN + tiles * N_EXPERTS + N_EXPERTS * (cb - 1)
    n_blocks = -(-max_chunks // cb) + 1
    per_seg = jnp.stack([seg_start, seg_len, seg_src], axis=1)
    delta = per_seg - jnp.concatenate([jnp.zeros((1, 3), jnp.int32), per_seg[:-1]], axis=0)
    filled = jnp.cumsum(jnp.zeros((n_blocks * cb, 3), jnp.int32).at[seg_start].add(delta, mode='drop'), axis=0)
    c = jnp.arange(n_blocks * cb, dtype=jnp.int32) - filled[:, 0]
    used = c < filled[:, 1]
    spare0 = tiles * (TILE_ROWS // SEG_ALIGN)
    k = jnp.arange(n_blocks * cb, dtype=jnp.int32)
    read_src = jnp.where(used, filled[:, 2] + c, spare0 + 2 * cb).astype(jnp.int32)
    write_dst = jnp.where(used, filled[:, 2] + c, spare0 + k % (2 * cb)).astype(jnp.int32)
    blk = jnp.arange(n_blocks, dtype=jnp.int32) * cb
    blk_e = jnp.minimum(jnp.sum((expert_end[None, :] <= blk[:, None]).astype(jnp.int32), axis=1), N_EXPERTS - 1)
    n_active = (expert_end[-1] // cb).astype(jnp.int32).reshape(1)
    return blk_e, n_active, read_src, write_dst


def _expert_kernel(blk_e_ref, nact_ref, rsrc_ref, wdst_ref, xs_hbm, wg_ref, wu_ref, wd_ref, out_hbm,
                   xbuf, obuf, gsem, ssem):
    del blk_e_ref
    b = pl.program_id(0)
    n_active = nact_ref[0]
    cb = CHUNKS_PER_BLOCK

    def chunk_copies(blk, slot, gather, start):
        for k in range(cb):
            local = pl.ds(k * SEG_ALIGN, SEG_ALIGN)
            if gather:
                ch = rsrc_ref[blk * cb + k]
                rows = pl.ds(pl.multiple_of(ch * SEG_ALIGN, SEG_ALIGN), SEG_ALIGN)
                cp = pltpu.make_async_copy(xs_hbm.at[rows], xbuf.at[slot, local], gsem.at[slot])
            else:
                ch = wdst_ref[blk * cb + k]
                rows = pl.ds(pl.multiple_of(ch * SEG_ALIGN, SEG_ALIGN), SEG_ALIGN)
                cp = pltpu.make_async_copy(obuf.at[slot, local], out_hbm.at[rows], ssem.at[slot])
            if start:
                cp.start()
            else:
                cp.wait()

    @pl.when(b == 0)
    def _():
        chunk_copies(0, 0, True, True)

    @pl.when(b < n_active)
    def _():
        slot = b % 2
        nxt = b + 1
        chunk_copies(b, slot, True, False)

        @pl.when(b >= 2)
        def _():
            chunk_copies(b - 2, slot, False, False)

        chunk_copies(nxt, 1 - slot, True, True)
        half = EXPERT_BLOCK // 2
        halves = [pl.ds(r * half, half) for r in range(2)]
        xs = [_unpack_pairs(xbuf[slot, rs, 0:PACK_W]) for rs in halves]
        gate = [_dot(x, wg_ref[0]) for x in xs]
        up = [_dot(x, wu_ref[0]) for x in xs]
        hid = []
        for rs, g, u in zip(halves, gate, up):
            w = pltpu.bitcast(xbuf[slot, rs, PACK_W:ROW_W], F32)
            hid.append((_silu(g) * u * jnp.concatenate([w, w], axis=1)).astype(BF16))
        outs = [_dot(h, wd_ref[0]) for h in hid]
        for rs, o in zip(halves, outs):
            obuf[slot, rs, 0:PACK_W] = _pack_pairs(o)
            obuf[slot, rs, PACK_W:ROW_W] = xbuf[slot, rs, PACK_W:ROW_W]
        chunk_copies(b, slot, False, True)

        @pl.when(b == n_active - 1)
        def _():
            chunk_copies(nxt, 1 - slot, True, False)
            chunk_copies(b, slot, False, False)

            @pl.when(b >= 1)
            def _():
                chunk_copies(b - 1, 1 - slot, False, False)


def _experts(xs, plan, wg, wu, wd):
    blk_e, n_active, read_src, write_dst = plan
    n_blocks = blk_e.shape[0]
    d = wg.shape[1]
    wspec = lambda shape: pl.BlockSpec((1,) + shape, lambda b, be, na, rs, wd_: (be[b], 0, 0))
    return pl.pallas_call(
        _expert_kernel,
        grid_spec=pltpu.PrefetchScalarGridSpec(
            num_scalar_prefetch=4,
            grid=(n_blocks,),
            in_specs=[pl.BlockSpec(memory_space=pl.ANY),
                      wspec((d, EXPERT_DIM)), wspec((d, EXPERT_DIM)), wspec((EXPERT_DIM, d))],
            out_specs=pl.BlockSpec(memory_space=pl.ANY),
            scratch_shapes=[pltpu.VMEM((2, EXPERT_BLOCK, ROW_W), xs.dtype),
                            pltpu.VMEM((2, EXPERT_BLOCK, ROW_W), xs.dtype),
                            pltpu.SemaphoreType.DMA((2,)), pltpu.SemaphoreType.DMA((2,))]),
        out_shape=jax.ShapeDtypeStruct(xs.shape, xs.dtype),
        input_output_aliases={4: 0},
        compiler_params=_cparams(("arbitrary",)),
        name="moe_experts",
    )(blk_e, n_active, read_src, write_dst, xs, wg, wu, wd)


def _combine_kernel(ys_ref, wtok_ref, h_ref, x1_ref, mod_ref, ln_ref, il_ref, su_ref,
                    sg_ref, su2_ref, sd_ref, o_ref, *, alpha):
    tm = h_ref.shape[0]
    r1 = ys_ref.shape[0]
    chosen = wtok_ref[...] > 0.0
    mask = jnp.where(chosen, 1.0, 0.0).astype(BF16)
    cnt = _dot(il_ref[...], mask)
    len_row = _ceil_seg(_dot(jnp.ones((8, tm), BF16), mask))
    off_row = _dot(len_row.astype(BF16), su_ref[...])
    dhi, dlo = _split_256(jnp.where(chosen, cnt + off_row[0:1, :] - 1.0, -1.0))
    len_col = _ceil_seg(_dot_tn(mask, jnp.ones((tm, EXPERT_LANES), BF16)))
    off_col = _dot_tn(su_ref[...], len_col.astype(BF16))
    reps = r1 // EXPERT_LANES
    lo = jnp.concatenate([off_col] * reps, axis=1)
    hi = lo + jnp.concatenate([len_col] * reps, axis=1)
    rows_e = lax.broadcasted_iota(jnp.int32, (EXPERT_LANES, r1), 1).astype(F32)
    segb = jnp.where(rows_e >= lo, jnp.where(rows_e < hi, 1.0, 0.0), 0.0).astype(BF16)
    want = 256.0 * _dot(dhi, segb) + _dot(dlo, segb)
    rows_t = lax.broadcasted_iota(jnp.int32, (tm, r1), 1).astype(F32)
    onehot = jnp.where(want == rows_t, 1.0, 0.0).astype(BF16)
    y = _dot(onehot, _unpack_pairs(ys_ref[:, 0:PACK_W]))
    x = h_ref[...]
    y = y + _dot((_silu(_dot(x, sg_ref[...])) * _dot(x, su2_ref[...])).astype(BF16), sd_ref[...])
    gate2 = mod_ref[0, 0, 5:6, :]
    o_ref[...] = _ln(alpha * x1_ref[...] + gate2 * y) * ln_ref[0:1, :] + ln_ref[1:2, :]


def _combine(ys, w_tok, h2, x1, mods, ln2, sg, su, sd, alpha, tiles_per_seq, n_ctx_tiles):
    n, d = h2.shape
    tiles = n // ROW_TILE
    iu, sup = _moe_tables(ROW_TILE)
    full = lambda arr: pl.BlockSpec(arr.shape, lambda i: (0,) * arr.ndim)
    row = pl.BlockSpec((ROW_TILE, d), lambda i: (i, 0))
    il = jnp.asarray(iu.T, BF16)
    sup = jnp.asarray(sup, BF16)
    return pl.pallas_call(
        functools.partial(_combine_kernel, alpha=alpha),
        grid=(tiles,),
        in_specs=[pl.BlockSpec((TILE_ROWS, ROW_W), lambda i: (i, 0)),
                  pl.BlockSpec((ROW_TILE, EXPERT_LANES), lambda i: (i, 0)),
                  row, row,
                  pl.BlockSpec((1, 1, 6, d), lambda i: (i // tiles_per_seq,
                                                        (i % tiles_per_seq >= n_ctx_tiles).astype(jnp.int32), 0, 0)),
                  full(ln2), full(il), full(sup), full(sg), full(su), full(sd)],
        out_specs=row,
        out_shape=jax.ShapeDtypeStruct((n, d), F32),
        compiler_params=_cparams(("parallel",)),
        name="moe_combine",
    )(ys, w_tok, h2, x1, mods, ln2, il, sup, sg, su, sd)


def _rope_tables(n_lat, n_ctx):
    rows = n_lat // GRID_W
    row = jnp.repeat(jnp.arange(rows, dtype=F32), GRID_W)
    col = (jnp.arange(n_lat) % GRID_W).astype(F32)
    n_freq = DA_DH // 4
    inv = ROPE_THETA ** (-jnp.arange(n_freq, dtype=F32) / n_freq)
    ang = jnp.concatenate([row[:, None] * inv, col[:, None] * inv], axis=-1)
    ang = jnp.concatenate([ang, ang], axis=-1)
    cos, sin = jnp.cos(ang), jnp.sin(ang)
    sign = jnp.where(jnp.arange(DA_DH) < DA_DH // 2, -1.0, 1.0).astype(F32)
    cos = jnp.tile(cos, (1, HEAD_W // DA_DH))
    sin = jnp.tile(sin * sign, (1, HEAD_W // DA_DH))
    cos = jnp.concatenate([jnp.ones((n_ctx, HEAD_W), F32), cos], axis=0)
    sin = jnp.concatenate([jnp.zeros((n_ctx, HEAD_W), F32), sin], axis=0)
    return cos, sin


def kernel(x, c, ctx, c_ctx, w_ada, b_ada, w_in, lb_logits, hg_norm_g, lam_params, da_norm_g, w_branch_a, w_branch_b, w_branch_c, w_out, ln1_g, ln1_b, ln2_g, ln2_b, w_router, router_bias, w_exp_gate, w_exp_up, w_exp_down, w_sh_gate, w_sh_up, w_sh_down):
    bsz, n_lat, d = x.shape
    n_ctx = ctx.shape[1]
    depth = w_in.shape[0]
    t = n_ctx + n_lat
    assert n_ctx % ROW_TILE == 0 and n_lat % ROW_TILE == 0
    assert n_ctx % HG_CHUNK == 0 and n_lat % HG_CHUNK == 0 and n_lat % GRID_W == 0
    n_ctx_tiles = n_ctx // ROW_TILE
    alpha = (2 * depth) ** 0.25
    per = N_EXPERTS // N_GROUPS

    cos, sin = _rope_tables(n_lat, n_ctx)
    sm = jax.nn.softmax(lb_logits.astype(F32), axis=0)
    lower = jnp.cumsum(sm, axis=0) - sm[0:1]
    log_lb = jnp.log(jnp.maximum(lower, LB_FLOOR))
    log1m_lb = jnp.log1p(-lower)

    n_rows = -(-(bsz + 1) // 8) * 8
    cvec = jnp.concatenate([c, c_ctx[None, :], jnp.zeros((n_rows - bsz - 1, d), F32)], axis=0)

    n_branch_cols = w_in.shape[2] - 3 * d

    def member_major(a):
        return a.reshape((N_GROUPS, per) + a.shape[1:]).swapaxes(0, 1).reshape(a.shape)

    xs = jnp.concatenate([ctx, x], axis=1)
    for l in range(depth):
        mod = _ada(cvec, w_ada[l], b_ada[l])
        mod_lat = mod[:bsz].reshape(bsz, 1, 6, d)
        mod_ctx = jnp.broadcast_to(mod[bsz].reshape(1, 1, 6, d), (bsz, 1, 6, d))
        mods = jnp.concatenate([mod_ctx, mod_lat], axis=1)

        w_in_l = jnp.concatenate([w_in[l][:, n_branch_cols:], w_in[l][:, :n_branch_cols]], axis=1)
        p = _inproj(xs, mods, w_in_l.astype(BF16), n_ctx)
        lb_f = jnp.stack([log_lb[l, 0], log1m_lb[l, 0]])
        lb_b = jnp.stack([log_lb[l, 1], log1m_lb[l, 1]])
        o_f, o_b = _hgrn(p, lb_f, lb_b, n_ctx)

        lam_init = 0.8 - 0.6 * math.exp(-0.3 * l)
        lp = lam_params[l].astype(F32)
        lam = jnp.exp(jnp.sum(lp[0] * lp[1])) - jnp.exp(jnp.sum(lp[2] * lp[3])) + lam_init
        qr, kr, vr = _attn_prep(p, cos, sin)
        at = _attention(qr, kr, vr, jnp.full((8, HEAD_W), lam, F32), n_ctx)
        fr = _fourier(p, n_ctx)

        skip = n_ctx_tiles if l == depth - 1 else 0
        ln1 = jnp.stack([ln1_g[l], ln1_b[l]])
        hg_g = hg_norm_g[l].reshape(1, HEAD_W)
        da_g = da_norm_g[l].reshape(1, HEAD_W)
        x1, h2, logits_t = _merge(
            xs, mods, p, o_f, o_b, at, fr, hg_g, da_g,
            w_branch_a[l].astype(BF16), w_branch_b[l].astype(BF16), w_branch_c[l].astype(BF16),
            w_out[l].astype(BF16), ln1, member_major(w_router[l].T), alpha, 1.0 - lam_init,
            n_ctx_tiles, skip)

        bias_col = jnp.broadcast_to(member_major(router_bias[l])[:, None], (N_EXPERTS, ROUTER_TILE))
        wts_t = _router(logits_t, bias_col)
        t_out = x1.shape[1]
        n_tok = bsz * t_out
        w_tok = jnp.concatenate([wts_t.T, jnp.zeros((n_tok, EXPERT_LANES - N_EXPERTS), F32)], axis=1)
        h2f = h2.reshape(n_tok, d)
        rows = _dispatch(h2f, wts_t, w_tok)
        plan = _moe_plan(wts_t, n_tok // ROW_TILE)
        rows = _experts(rows, plan, member_major(w_exp_gate[l]).astype(BF16),
                        member_major(w_exp_up[l]).astype(BF16), member_major(w_exp_down[l]).astype(BF16))
        ln2 = jnp.stack([ln2_g[l], ln2_b[l]])
        xs = _combine(rows, w_tok, h2f, x1.reshape(n_tok, d), mods, ln2,
                      w_sh_gate[l].astype(BF16), w_sh_up[l].astype(BF16), w_sh_down[l].astype(BF16),
                      alpha, t_out // ROW_TILE, n_ctx_tiles - skip).reshape(bsz, t_out, d)
    return xs if xs.shape[1] == n_lat else xs[:, n_ctx:]
```

```python
import functools
import math

import numpy as np
import jax
import jax.numpy as jnp
from jax import lax
from jax.experimental import pallas as pl
from jax.experimental.pallas import tpu as pltpu

F32 = jnp.float32
BF16 = jnp.bfloat16

HG_HEADS = 4
HG_DK = 128
LB_FLOOR = 1e-30
DA_HEADS = 4
DA_DH = 64
ROPE_THETA = 10000.0
GRID_W = 64
FR_GROUPS = 4
FR_DG = 128
N_EXPERTS = 64
TOP_K = 8
EXPERT_DIM = 256
N_GROUPS = 8
TOPK_GROUPS = 4
ROUTED_SCALE = 2.5
MASK_VALUE = -1e9
LN_EPS = 1e-5
LOG2_E = 1.4426950408889634

HEAD_W = 128
BRANCH_W = 512
GATES_BLOCKS = 6
COL_Q, COL_FF, COL_FB, COL_I, COL_OG, COL_QA, COL_KA, COL_VA, COL_U = range(6, 15)

ROW_TILE = 256
HG_CHUNK = 128
HG_LEVELS = (64, 32, 16, 8, 4, 2, 1)
HG_ROW_REF_MIN = 8
IN_N_TILE = 2560
IN_ROW_TILE = 768
ROUTER_TILE = 512
VMEM_LIMIT = 48 * 1024 * 1024


def _cparams(sem):
    return pltpu.CompilerParams(dimension_semantics=sem, vmem_limit_bytes=VMEM_LIMIT)


def _dot(a, b):
    return jnp.dot(a, b, preferred_element_type=F32)


def _dot_nt(a, b):
    return lax.dot_general(a, b, (((1,), (1,)), ((), ())), preferred_element_type=F32)


def _dot_tn(a, b):
    return lax.dot_general(a, b, (((0,), (0,)), ((), ())), preferred_element_type=F32)


def _silu(x):
    return x * jax.nn.sigmoid(x)


def _ln(x):
    mu = jnp.mean(x, axis=-1, keepdims=True)
    xc = x - mu
    var = jnp.mean(xc * xc, axis=-1, keepdims=True)
    return xc * lax.rsqrt(var + LN_EPS)


def _ada_kernel(c_ref, w_ref, b_ref, o_ref):
    o_ref[...] = jnp.dot(_silu(c_ref[...]), w_ref[...], preferred_element_type=F32,
                         precision=lax.Precision.HIGHEST) + b_ref[...]


def _ada(cvec, w, b):
    rows, d = cvec.shape
    n = w.shape[1]
    tn = 1536
    return pl.pallas_call(
        _ada_kernel,
        grid=(n // tn,),
        in_specs=[pl.BlockSpec((rows, d), lambda j: (0, 0)),
                  pl.BlockSpec((d, tn), lambda j: (0, j)),
                  pl.BlockSpec((1, tn), lambda j: (0, j))],
        out_specs=pl.BlockSpec((rows, tn), lambda j: (0, j)),
        out_shape=jax.ShapeDtypeStruct((rows, n), F32),
        compiler_params=_cparams(("parallel",)),
        name="ada",
    )(cvec, w, b.reshape(1, n))


def _inproj_kernel(x_ref, mod_ref, w_ref, o_ref, *, n_ctx):
    first = pl.program_id(2) == 0
    xn = _ln(x_ref[0])
    shift_top = jnp.where(first, mod_ref[0, 0, 0:1, :], mod_ref[0, 1, 0:1, :])
    scale_top = jnp.where(first, mod_ref[0, 0, 1:2, :], mod_ref[0, 1, 1:2, :])
    parts = [xn[:n_ctx] * (1.0 + scale_top) + shift_top]
    if xn.shape[0] > n_ctx:
        parts.append(xn[n_ctx:] * (1.0 + mod_ref[0, 1, 1:2, :]) + mod_ref[0, 1, 0:1, :])
    h = jnp.concatenate(parts, axis=0) if len(parts) > 1 else parts[0]
    o_ref[0] = _dot(h.astype(BF16), w_ref[...])


def _inproj(x, mods, w, n_ctx):
    bsz, t, d = x.shape
    n = w.shape[1]
    tm = IN_ROW_TILE if t % IN_ROW_TILE == 0 and n_ctx <= IN_ROW_TILE else n_ctx
    return pl.pallas_call(
        functools.partial(_inproj_kernel, n_ctx=n_ctx),
        grid=(n // IN_N_TILE, bsz, t // tm),
        in_specs=[pl.BlockSpec((1, tm, d), lambda j, b, i: (b, i, 0)),
                  pl.BlockSpec((1, 2, 6, d), lambda j, b, i: (b, 0, 0, 0)),
                  pl.BlockSpec((d, IN_N_TILE), lambda j, b, i: (0, j))],
        out_specs=pl.BlockSpec((1, tm, IN_N_TILE), lambda j, b, i: (b, i, j)),
        out_shape=jax.ShapeDtypeStruct((bsz, t, n), F32),
        compiler_params=_cparams(("parallel", "parallel", "parallel")),
        name="inproj",
    )(x, mods, w)


def _hgrn_tables(reverse):
    c = HG_CHUNK
    t = np.arange(c)
    if reverse:
        cum = (t[None, :] >= t[:, None])
    else:
        cum = (t[None, :] <= t[:, None])
    mats = [cum]
    for m in HG_LEVELS:
        if m < HG_ROW_REF_MIN:
            idx = (t // (2 * m)) * (2 * m) + m - (1 if reverse else 0)
            mats.append(cum[idx])
    stack = np.concatenate(mats, axis=0).astype(np.float32)
    x = t[:, None] ^ t[None, :]
    lvl = np.zeros((c, c), np.int32)
    for li, m in enumerate(HG_LEVELS):
        lvl = np.where((x >= m) & (x < 2 * m), li + 1, lvl)
    visible = (t[None, :] >= t[:, None]) if reverse else (t[None, :] <= t[:, None])
    lvl = np.where(visible, lvl, -1).astype(np.int32)
    return stack, lvl


def _hgrn_kernel(qf_ref, ff_ref, vf_ref, qb_ref, fb_ref, vb_ref, lbf_ref, lbb_ref,
                 stf_ref, lvf_ref, stb_ref, lvb_ref, of_ref, ob_ref, sf_ref, sb_ref):
    c = HG_CHUNK

    @pl.when(pl.program_id(1) == 0)
    def _():
        sf_ref[...] = jnp.zeros_like(sf_ref)
        sb_ref[...] = jnp.zeros_like(sb_ref)

    chains = []
    for h in range(HG_HEADS):
        chains.append((qf_ref, ff_ref, vf_ref, lbf_ref, stf_ref, lvf_ref, of_ref, sf_ref, h, False))
        chains.append((qb_ref, fb_ref, vb_ref, lbb_ref, stb_ref, lvb_ref, ob_ref, sb_ref, h, True))

    gates = []
    for q_ref, f_ref, v_ref, lb_ref, _, _, _, _, h, _ in chains:
        sl = slice(h * HEAD_W, (h + 1) * HEAD_W)
        q = _silu(q_ref[0, :, sl])
        fr = f_ref[0, :, sl]
        v = v_ref[0, :, sl].astype(BF16)
        log_lb = lb_ref[0:1, sl]
        log1m_lb = lb_ref[1:2, sl]
        c2 = log1m_lb + jnp.minimum(fr, 0.0) - jnp.log1p(jnp.exp(-jnp.abs(fr)))
        g = jnp.maximum(log_lb, c2) + jnp.log1p(jnp.exp(-jnp.abs(log_lb - c2)))
        gates.append((q, 1.0 - jnp.exp(g), v, g))

    cums = []
    for (_, _, _, _, stack_ref, _, _, _, _, _), (_, _, _, g) in zip(chains, gates):
        g1 = g.astype(BF16)
        r1 = g - g1.astype(F32)
        g2 = r1.astype(BF16)
        g3 = (r1 - g2.astype(F32)).astype(BF16)
        stack = stack_ref[...]
        cums.append(_dot(stack, g1) + _dot(stack, g2) + _dot(stack, g3))

    acc = []
    qk16 = [(q.astype(BF16), k.astype(BF16)) for q, k, _, _ in gates]
    for (_, _, _, _, _, lvl_ref, _, _, _, _), (q16, k16) in zip(chains, qk16):
        acc.append(jnp.where(lvl_ref[...] == 0, _dot_nt(q16, k16), 0.0))
    n_stacked = 0
    for li, m in enumerate(HG_LEVELS):
        if m < HG_ROW_REF_MIN:
            n_stacked += 1
        for ci, (chain, (q16, k16), bb) in enumerate(zip(chains, qk16, cums)):
            reverse = chain[9]
            b = bb[0:c]
            if m >= HG_ROW_REF_MIN:
                rows = []
                for j in range(c // (2 * m)):
                    idx = j * 2 * m + m - (1 if reverse else 0)
                    rows.append(jnp.broadcast_to(b[idx:idx + 1, :], (2 * m, HEAD_W)))
                ref = rows[0] if len(rows) == 1 else jnp.concatenate(rows, axis=0)
            else:
                ref = bb[n_stacked * c:(n_stacked + 1) * c]
            e = jnp.exp(-jnp.abs(b - ref)).astype(BF16)
            acc[ci] = jnp.where(chain[5][...] == li + 1, _dot_nt(q16 * e, k16 * e), acc[ci])

    for chain, (q, k, v, _), bb, a in zip(chains, gates, cums, acc):
        o_ref, st_ref, h, reverse = chain[6], chain[7], chain[8], chain[9]
        sl = slice(h * HEAD_W, (h + 1) * HEAD_W)
        b = bb[0:c]
        st = st_ref[h]
        o_ref[0, :, sl] = _dot(a.astype(BF16), v) + _dot_nt((q * jnp.exp(b)).astype(BF16), st.astype(BF16))
        b_end = b[0:1] if reverse else b[c - 1:c]
        k_end = (k * jnp.exp(b_end - b)).astype(BF16)
        st_ref[h] = st * jnp.exp(b_end) + _dot_tn(v, k_end)


def _hgrn(p, lb_f, lb_b, n_ctx):
    bsz, t, _ = p.shape
    c = HG_CHUNK
    nc, nt = n_ctx // c, t // c
    stack_f, lvl_f = _hgrn_tables(False)
    stack_b, lvl_b = _hgrn_tables(True)

    def rev(j):
        return jnp.where(j < nc, nc - 1 - j, nc + nt - 1 - j)

    nst = stack_f.shape[0]
    blk = lambda row, col: pl.BlockSpec((1, c, BRANCH_W), lambda b, j: (b, row(j), col))
    const = lambda shape: pl.BlockSpec(shape, lambda b, j: (0, 0))
    fwd = lambda j: j
    out = jax.ShapeDtypeStruct((bsz, t, BRANCH_W), F32)
    state = pltpu.VMEM((HG_HEADS, HEAD_W, HEAD_W), F32)
    return pl.pallas_call(
        _hgrn_kernel,
        grid=(bsz, nt),
        in_specs=[blk(fwd, COL_Q), blk(fwd, COL_FF), blk(fwd, COL_I),
                  blk(rev, COL_Q), blk(rev, COL_FB), blk(rev, COL_I),
                  const((2, BRANCH_W)), const((2, BRANCH_W)),
                  const((nst, c)), const((c, c)), const((nst, c)), const((c, c))],
        out_specs=[blk(fwd, 0), blk(rev, 0)],
        out_shape=[out, out],
        scratch_shapes=[state, state],
        compiler_params=_cparams(("parallel", "arbitrary")),
        name="hgrn",
    )(p, p, p, p, p, p, lb_f, lb_b, jnp.asarray(stack_f, BF16), jnp.asarray(lvl_f),
      jnp.asarray(stack_b, BF16), jnp.asarray(lvl_b))


def _attn_prep_kernel(q_ref, k_ref, v_ref, cos_ref, sin_ref, qo_ref, ko_ref, vo_ref):
    cos = cos_ref[...]
    sin = sin_ref[...]
    lane = lax.broadcasted_iota(jnp.int32, cos.shape, 1)
    first_half = (lane % DA_DH) < (DA_DH // 2)

    def rope(x):
        up = pltpu.roll(x, HEAD_W - DA_DH // 2, 1)
        down = pltpu.roll(x, DA_DH // 2, 1)
        return x * cos + jnp.where(first_half, up, down) * sin

    ones = jnp.ones((q_ref.shape[1], HEAD_W), BF16)
    for h in range(DA_HEADS):
        sl = slice(h * HEAD_W, (h + 1) * HEAD_W)
        qo_ref[0, :, sl] = (rope(q_ref[0, :, sl]) * (DA_DH ** -0.5 * LOG2_E)).astype(BF16)
        ko_ref[0, :, sl] = rope(k_ref[0, :, sl]).astype(BF16)
        vo_ref[0, :, 2 * h * HEAD_W:(2 * h + 1) * HEAD_W] = v_ref[0, :, sl].astype(BF16)
        vo_ref[0, :, (2 * h + 1) * HEAD_W:(2 * h + 2) * HEAD_W] = ones


def _attn_prep(p, cos, sin):
    bsz, t, _ = p.shape
    spec = lambda col: pl.BlockSpec((1, ROW_TILE, BRANCH_W), lambda b, i: (b, i, col))
    tab = pl.BlockSpec((ROW_TILE, HEAD_W), lambda b, i: (i, 0))
    out = jax.ShapeDtypeStruct((bsz, t, BRANCH_W), BF16)
    return pl.pallas_call(
        _attn_prep_kernel,
        grid=(bsz, t // ROW_TILE),
        in_specs=[spec(COL_QA), spec(COL_KA), spec(COL_VA), tab, tab],
        out_specs=[spec(0), spec(0), pl.BlockSpec((1, ROW_TILE, 2 * BRANCH_W), lambda b, i: (b, i, 0))],
        out_shape=[out, out, jax.ShapeDtypeStruct((bsz, t, 2 * BRANCH_W), BF16)],
        compiler_params=_cparams(("parallel", "parallel")),
        name="attn_prep",
    )(p, p, p, cos, sin)


def _attn_kernel(q_ref, k_ref, v_ref, lam_ref, o_ref, *, n_ctx, n_ctx_tiles):
    lam = lam_ref[0:1, 0:1]

    def attend(n_keys):
        k = k_ref[0, 0:n_keys, :]
        v = v_ref[0, 0:n_keys, :]
        half = q_ref.shape[1] // 2
        lane = lax.broadcasted_iota(jnp.int32, (half, HEAD_W), 1)
        chains = [(r, first) for r in range(2) for first in (True, False)]
        scores = []
        for r, first in chains:
            q = q_ref[0, r * half:(r + 1) * half, :]
            qm = jnp.where((lane < DA_DH) == first, q, jnp.zeros_like(q))
            scores.append(_dot_nt(qm, k))
        probs = [jnp.exp2(s - jnp.max(s, axis=-1, keepdims=True)).astype(BF16) for s in scores]
        outs = []
        for e in probs:
            pv = _dot(e, v)
            outs.append(pv[:, 0:HEAD_W] / pv[:, HEAD_W:2 * HEAD_W])
        for r in range(2):
            o_ref[0, r * half:(r + 1) * half, :] = outs[2 * r] - lam * outs[2 * r + 1]

    is_ctx = pl.program_id(2) < n_ctx_tiles

    @pl.when(is_ctx)
    def _():
        attend(n_ctx)

    @pl.when(jnp.logical_not(is_ctx))
    def _():
        attend(k_ref.shape[1])


def _attention(q, k, v, lam, n_ctx):
    bsz, t, _ = q.shape
    n_ctx_tiles = n_ctx // ROW_TILE
    return pl.pallas_call(
        functools.partial(_attn_kernel, n_ctx=n_ctx, n_ctx_tiles=n_ctx_tiles),
        grid=(bsz, DA_HEADS, t // ROW_TILE),
        in_specs=[pl.BlockSpec((1, ROW_TILE, HEAD_W), lambda b, h, i: (b, i, h)),
                  pl.BlockSpec((1, t, HEAD_W), lambda b, h, i: (b, 0, h)),
                  pl.BlockSpec((1, t, 2 * HEAD_W), lambda b, h, i: (b, 0, h)),
                  pl.BlockSpec((8, HEAD_W), lambda b, h, i: (0, 0))],
        out_specs=pl.BlockSpec((1, ROW_TILE, HEAD_W), lambda b, h, i: (b, i, h)),
        out_shape=jax.ShapeDtypeStruct((bsz, t, BRANCH_W), F32),
        compiler_params=_cparams(("parallel", "parallel", "parallel")),
        name="diff_attn",
    )(q, k, v, lam)


def _dft_tables(n):
    idx = np.arange(n, dtype=np.int64)
    ang = 2.0 * np.pi * ((idx[:, None] * idx[None, :]) % n).astype(np.float64) / n
    scale = 1.0 / math.sqrt(n * FR_DG)
    return np.concatenate([np.cos(ang), -np.sin(ang)], axis=1).astype(np.float32) * np.float32(scale)


def _channel_dft_tables():
    idx = np.arange(FR_DG, dtype=np.int64)
    ang = 2.0 * np.pi * ((idx[:, None] * idx[None, :]) % FR_DG).astype(np.float64) / FR_DG
    eye = np.eye(FR_GROUPS)
    return (np.kron(eye, np.cos(ang)).astype(np.float32), np.kron(eye, np.sin(ang)).astype(np.float32))


def _fourier_kernel(u_ref, cc_ref, sc_ref, wc_ref, wl_ref, o_ref, zc_ref, zl_ref, *, n_ctx, n_ctx_tiles):
    i = pl.program_id(1)
    n_lat = u_ref.shape[1] - n_ctx

    @pl.when(i == 0)
    def _():
        u = u_ref[0].astype(BF16)
        zcos = _dot(u, cc_ref[...]).astype(BF16)
        zsin = _dot(u, sc_ref[...]).astype(BF16)
        zc_ref[0:n_ctx] = zcos[0:n_ctx]
        zc_ref[n_ctx:2 * n_ctx] = zsin[0:n_ctx]
        zl_ref[0:n_lat] = zcos[n_ctx:]
        zl_ref[n_lat:2 * n_lat] = zsin[n_ctx:]

    @pl.when(i < n_ctx_tiles)
    def _():
        o_ref[0] = _dot(wc_ref[...], zc_ref[...])

    @pl.when(i >= n_ctx_tiles)
    def _():
        o_ref[0] = _dot(wl_ref[...], zl_ref[...])


def _fourier(p, n_ctx):
    bsz, t, _ = p.shape
    n_lat = t - n_ctx
    n_ctx_tiles = n_ctx // ROW_TILE
    cc, sc = _channel_dft_tables()
    wc = jnp.asarray(_dft_tables(n_ctx)).astype(BF16)
    wl = jnp.asarray(_dft_tables(n_lat)).astype(BF16)
    cc = jnp.asarray(cc).astype(BF16)
    sc = jnp.asarray(sc).astype(BF16)
    return pl.pallas_call(
        functools.partial(_fourier_kernel, n_ctx=n_ctx, n_ctx_tiles=n_ctx_tiles),
        grid=(bsz, t // ROW_TILE),
        in_specs=[pl.BlockSpec((1, t, BRANCH_W), lambda b, i: (b, 0, COL_U)),
                  pl.BlockSpec((BRANCH_W, BRANCH_W), lambda b, i: (0, 0)),
                  pl.BlockSpec((BRANCH_W, BRANCH_W), lambda b, i: (0, 0)),
                  pl.BlockSpec((ROW_TILE, 2 * n_ctx), lambda b, i: (jnp.minimum(i, n_ctx_tiles - 1), 0)),
                  pl.BlockSpec((ROW_TILE, 2 * n_lat), lambda b, i: (jnp.maximum(i - n_ctx_tiles, 0), 0))],
        out_specs=pl.BlockSpec((1, ROW_TILE, BRANCH_W), lambda b, i: (b, i, 0)),
        out_shape=jax.ShapeDtypeStruct((bsz, t, BRANCH_W), F32),
        scratch_shapes=[pltpu.VMEM((2 * n_ctx, BRANCH_W), BF16), pltpu.VMEM((2 * n_lat, BRANCH_W), BF16)],
        compiler_params=_cparams(("parallel", "arbitrary")),
        name="fourier",
    )(p, cc, sc, wc, wl)


def _head_rms(x, gain):
    parts = []
    for h in range(BRANCH_W // HEAD_W):
        xh = x[:, h * HEAD_W:(h + 1) * HEAD_W]
        parts.append(xh * lax.rsqrt(jnp.mean(xh * xh, axis=-1, keepdims=True) + LN_EPS) * gain)
    return jnp.concatenate(parts, axis=-1)


def _merge_kernel(x_ref, mod_ref, gates_ref, og_ref, of_ref, ob_ref, at_ref, fr_ref,
                  hg_ref, da_ref, wa_ref, wb_ref, wc_ref, wo_ref, ln_ref, wr_ref,
                  x1_ref, h2_ref, lg_ref, *, alpha, attn_scale):
    d = x_ref.shape[2]
    tm = x_ref.shape[1]
    gate1 = mod_ref[0, 0, 2:3, :]
    shift2 = mod_ref[0, 0, 3:4, :]
    scale2 = mod_ref[0, 0, 4:5, :]
    halves = [slice(r * (tm // 2), (r + 1) * (tm // 2)) for r in range(2)]
    a = [(_head_rms(of_ref[0, rs, :] + ob_ref[0, rs, :], hg_ref[...]) * _silu(og_ref[0, rs, :])).astype(BF16)
         for rs in halves]
    bq = [(_head_rms(at_ref[0, rs, :], da_ref[...]) * attn_scale).astype(BF16) for rs in halves]
    pa = [_dot(v, wa_ref[...]) for v in a]
    pb = [_dot(v, wb_ref[...]) for v in bq]
    pc = [_dot(fr_ref[0, rs, :].astype(BF16), wc_ref[...]) for rs in halves]
    m = []
    for rs, ya, yb, yc in zip(halves, pa, pb, pc):
        m.append((jax.nn.sigmoid(gates_ref[0, rs, 0:d]) * ya + jax.nn.sigmoid(gates_ref[0, rs, d:2 * d]) * yb
                  + jax.nn.sigmoid(gates_ref[0, rs, 2 * d:3 * d]) * yc).astype(BF16))
    mix = [_dot(v, wo_ref[...]) for v in m]
    h2s = []
    for rs, mx in zip(halves, mix):
        x1 = _ln(alpha * x_ref[0, rs, :] + gate1 * mx) * ln_ref[0:1, :] + ln_ref[1:2, :]
        x1_ref[0, rs, :] = x1
        h2 = _ln(x1) * (1.0 + scale2) + shift2
        h2_ref[0, rs, :] = h2.astype(BF16)
        h2s.append(h2)
    for rs, h2 in zip(halves, h2s):
        lg_ref[:, rs] = lax.dot_general(wr_ref[...], h2, (((1,), (1,)), ((), ())),
                                        preferred_element_type=F32, precision=lax.Precision.HIGHEST)


def _merge(x, mods, p, o_f, o_b, at, fr, hg_g, da_g, wa, wb, wc, wo, ln1, wr_t, alpha, attn_scale,
           n_ctx_tiles, skip_tiles):
    bsz, t, d = x.shape
    tiles = t // ROW_TILE - skip_tiles
    t_out = tiles * ROW_TILE
    row = lambda w, col: pl.BlockSpec((1, ROW_TILE, w), lambda b, i: (b, i + skip_tiles, col))
    out_row = pl.BlockSpec((1, ROW_TILE, d), lambda b, i: (b, i, 0))
    full = lambda arr: pl.BlockSpec(arr.shape, lambda b, i: (0,) * arr.ndim)
    return pl.pallas_call(
        functools.partial(_merge_kernel, alpha=alpha, attn_scale=attn_scale),
        grid=(bsz, tiles),
        in_specs=[row(d, 0),
                  pl.BlockSpec((1, 1, 6, d),
                               lambda b, i: (b, (i + skip_tiles >= n_ctx_tiles).astype(jnp.int32), 0, 0)),
                  row(GATES_BLOCKS * BRANCH_W, 0), row(BRANCH_W, COL_OG),
                  row(BRANCH_W, 0), row(BRANCH_W, 0), row(BRANCH_W, 0), row(BRANCH_W, 0),
                  full(hg_g), full(da_g), full(wa), full(wb), full(wc), full(wo), full(ln1), full(wr_t)],
        out_specs=[out_row, out_row,
                   pl.BlockSpec((N_EXPERTS, ROW_TILE), lambda b, i: (0, b * tiles + i))],
        out_shape=[jax.ShapeDtypeStruct((bsz, t_out, d), F32),
                   jax.ShapeDtypeStruct((bsz, t_out, d), BF16),
                   jax.ShapeDtypeStruct((N_EXPERTS, bsz * t_out), F32)],
        compiler_params=_cparams(("parallel", "parallel")),
        name="merge",
    )(x, mods, p, p, o_f, o_b, at, fr, hg_g, da_g, wa, wb, wc, wo, ln1, wr_t)


def _router_kernel(lg_ref, bias_ref, w_ref):
    per = N_EXPERTS // N_GROUPS
    scores = [jax.nn.sigmoid(lg_ref[j * N_GROUPS:(j + 1) * N_GROUPS, :]) for j in range(per)]
    sel = [scores[j] + bias_ref[j * N_GROUPS:(j + 1) * N_GROUPS, :] for j in range(per)]
    shape = sel[0].shape
    grp = lax.broadcasted_iota(jnp.int32, shape, 0)

    def beats(other, me, tie):
        if isinstance(tie, bool):
            return jnp.where((other >= me) if tie else (other > me), 1, 0)
        return jnp.where(other > me, 1, jnp.where(other == me, tie, 0))

    m1 = jnp.maximum(sel[0], sel[1])
    m2 = jnp.minimum(sel[0], sel[1])
    for j in range(2, per):
        m2 = jnp.maximum(m2, jnp.minimum(m1, sel[j]))
        m1 = jnp.maximum(m1, sel[j])
    gscore = m1 + m2
    grank = jnp.zeros(shape, jnp.int32)
    lower_group = [None] + [jnp.where(grp >= kk, 1, 0) for kk in range(1, N_GROUPS)]
    for kk in range(1, N_GROUPS):
        other = pltpu.roll(gscore, kk, 0)
        grank = grank + beats(other, gscore, lower_group[kk])
    gmask = grank < TOPK_GROUPS
    sel = [jnp.where(gmask, s, MASK_VALUE) for s in sel]
    rolled = [[None] + [pltpu.roll(s, kk, 0) for kk in range(1, N_GROUPS)] for s in sel]
    weights = []
    for j in range(per):
        rank = jnp.zeros(shape, jnp.int32)
        for j2 in range(per):
            for kk in range(N_GROUPS):
                if kk == 0 and j2 == j:
                    continue
                if kk == 0:
                    rank = rank + beats(sel[j2], sel[j], j2 < j)
                else:
                    rank = rank + beats(rolled[j2][kk], sel[j], lower_group[kk])
        weights.append(jnp.where(rank < TOP_K, scores[j], 0.0))
    total = weights[0]
    for j in range(1, per):
        total = total + weights[j]
    total = jnp.sum(total, axis=0, keepdims=True)
    for j in range(per):
        w_ref[j * N_GROUPS:(j + 1) * N_GROUPS, :] = weights[j] / total * ROUTED_SCALE


def _router(logits_t, bias_col):
    e, n = logits_t.shape
    return pl.pallas_call(
        _router_kernel,
        grid=(n // ROUTER_TILE,),
        in_specs=[pl.BlockSpec((e, ROUTER_TILE), lambda i: (0, i)),
                  pl.BlockSpec((e, ROUTER_TILE), lambda i: (0, 0))],
        out_specs=pl.BlockSpec((e, ROUTER_TILE), lambda i: (0, i)),
        out_shape=jax.ShapeDtypeStruct((e, n), F32),
        compiler_params=_cparams(("parallel",)),
        name="router",
    )(logits_t, bias_col)


SEG_ALIGN = 8
TILE_ROWS = ROW_TILE * TOP_K + N_EXPERTS * SEG_ALIGN
CHUNKS_PER_BLOCK = 128
EXPERT_BLOCK = CHUNKS_PER_BLOCK * SEG_ALIGN
PACK_W = 512
ROW_W = PACK_W + HEAD_W
EXPERT_LANES = 128
HI_MASK = -65536


def _pack_pairs(x):
    hi = pltpu.bitcast(x[:, :PACK_W].astype(jnp.bfloat16).astype(F32), jnp.int32)
    lo = pltpu.bitcast(x[:, PACK_W:].astype(jnp.bfloat16).astype(F32), jnp.int32)
    return hi | lax.shift_right_logical(lo, jnp.full_like(lo, 16))


def _pack_bf16_valued_pairs(x):
    hi = pltpu.bitcast(x[:, :PACK_W], jnp.int32) & HI_MASK
    lo = pltpu.bitcast(x[:, PACK_W:], jnp.int32)
    return hi | lax.shift_right_logical(lo, jnp.full_like(lo, 16))


def _unpack_pairs(u):
    hi = pltpu.bitcast(u & HI_MASK, F32)
    lo = pltpu.bitcast(lax.shift_left(u, jnp.full_like(u, 16)), F32)
    return jnp.concatenate([hi, lo], axis=1).astype(BF16)


def _ceil_seg(c):
    return jnp.floor((c + (SEG_ALIGN - 1.0)) * (1.0 / SEG_ALIGN)) * SEG_ALIGN


def _split_256(c):
    hi = jnp.floor(c * (1.0 / 256.0))
    return hi.astype(BF16), (c - 256.0 * hi).astype(BF16)


def _moe_tables(tm):
    t = np.arange(tm)
    e = np.arange(EXPERT_LANES)
    incl_upper = (t[:, None] <= t[None, :]).astype(np.float32)
    strict_upper = (e[:, None] < e[None, :]).astype(np.float32)
    return incl_upper, strict_upper


def _dispatch_kernel(h_ref, wt_ref, wtok_ref, iu_ref, su_ref, o_ref):
    last = pl.num_programs(0) - 1

    @pl.when(pl.program_id(0) == last)
    def _():
        o_ref[...] = jnp.zeros_like(o_ref)

    @pl.when(pl.program_id(0) < last)
    def _():
        tm = h_ref.shape[0]
        r1 = o_ref.shape[0]
        sel = jnp.where(wt_ref[...] > 0.0, 1.0, 0.0)
        sel = jnp.concatenate([sel, jnp.zeros((EXPERT_LANES - N_EXPERTS, tm), F32)], axis=0)
        chosen = sel > 0.5
        mask = sel.astype(BF16)
        cnt = _dot(mask, iu_ref[...])
        ones_lanes = jnp.ones((tm, EXPERT_LANES), BF16)
        len_col = _ceil_seg(_dot(mask, ones_lanes))
        off_col = _dot_tn(su_ref[...], len_col.astype(BF16))
        len_row = _ceil_seg(_dot_nt(jnp.ones((8, tm), BF16), mask))
        off_row = _dot(len_row.astype(BF16), su_ref[...])
        dest = jnp.where(chosen, cnt + jnp.concatenate([off_col] * (tm // EXPERT_LANES), axis=1) - 1.0, -1.0)
        dhi, dlo = _split_256(dest)
        rows_e = lax.broadcasted_iota(jnp.int32, (r1, EXPERT_LANES), 0).astype(F32)
        lo = off_row[0:1, :]
        seg = jnp.where(rows_e >= lo, jnp.where(rows_e < lo + len_row[0:1, :], 1.0, 0.0), 0.0)
        segb = seg.astype(BF16)
        want = 256.0 * _dot(segb, dhi) + _dot(segb, dlo)
        rows_t = lax.broadcasted_iota(jnp.int32, (r1, tm), 0).astype(F32)
        onehot = jnp.where(want == rows_t, 1.0, 0.0).astype(BF16)
        xs = _dot(onehot, h_ref[...])
        wtok = wtok_ref[...]
        w_hi = wtok.astype(BF16)
        w_lo = (wtok - w_hi.astype(F32)).astype(BF16)
        w_row = jnp.sum((_dot(onehot, w_hi) + _dot(onehot, w_lo)) * seg, axis=1, keepdims=True)
        o_ref[:, 0:PACK_W] = _pack_bf16_valued_pairs(xs)
        o_ref[:, PACK_W:ROW_W] = pltpu.bitcast(jnp.broadcast_to(w_row, (r1, HEAD_W)), jnp.int32)


def _dispatch(h2, w_t, w_tok):
    n, d = h2.shape
    tiles = n // ROW_TILE
    iu, su = _moe_tables(ROW_TILE)
    last = tiles - 1
    return pl.pallas_call(
        _dispatch_kernel,
        grid=(tiles + 1,),
        in_specs=[pl.BlockSpec((ROW_TILE, d), lambda i: (jnp.minimum(i, last), 0)),
                  pl.BlockSpec((N_EXPERTS, ROW_TILE), lambda i: (0, jnp.minimum(i, last))),
                  pl.BlockSpec((ROW_TILE, EXPERT_LANES), lambda i: (jnp.minimum(i, last), 0)),
                  pl.BlockSpec((ROW_TILE, ROW_TILE), lambda i: (0, 0)),
                  pl.BlockSpec((EXPERT_LANES, EXPERT_LANES), lambda i: (0, 0))],
        out_specs=pl.BlockSpec((TILE_ROWS, ROW_W), lambda i: (i, 0)),
        out_shape=jax.ShapeDtypeStruct(((tiles + 1) * TILE_ROWS, ROW_W), jnp.int32),
        compiler_params=_cparams(("parallel",)),
        name="moe_dispatch",
    )(h2, w_t, w_tok, jnp.asarray(iu, BF16), jnp.asarray(su, BF16))


def _moe_plan(w_t, tiles):
    cb = CHUNKS_PER_BLOCK
    cnt = jnp.sum((w_t > 0.0).reshape(N_EXPERTS, tiles, ROW_TILE), axis=-1, dtype=jnp.int32)
    nch = (cnt + SEG_ALIGN - 1) // SEG_ALIGN
    tile_chunk0 = jnp.cumsum(nch, axis=0) - nch
    per_expert = jnp.sum(nch, axis=1)
    padded = (per_expert + cb - 1) // cb * cb
    expert_end = jnp.cumsum(padded)
    expert_start = expert_end - padded
    seg_start = (expert_start[:, None] + jnp.cumsum(nch, axis=1) - nch).reshape(-1)
    seg_len = nch.reshape(-1)
    seg_src = (jnp.arange(tiles, dtype=jnp.int32)[None, :] * (TILE_ROWS // SEG_ALIGN) + tile_chunk0).reshape(-1)
    n_tok = tiles * ROW_TILE
    max_chunks = n_tok * TOP_K // SEG_ALIGN + tiles * N_EXPERTS + N_EXPERTS * (cb - 1)
    n_blocks = -(-max_chunks // cb)
    per_seg = jnp.stack([seg_start, seg_len, seg_src], axis=1)
    delta = per_seg - jnp.concatenate([jnp.zeros((1, 3), jnp.int32), per_seg[:-1]], axis=0)
    filled = jnp.cumsum(jnp.zeros((n_blocks * cb, 3), jnp.int32).at[seg_start].add(delta, mode='drop'), axis=0)
    c = jnp.arange(n_blocks * cb, dtype=jnp.int32) - filled[:, 0]
    used = c < filled[:, 1]
    spare0 = tiles * (TILE_ROWS // SEG_ALIGN)
    k = jnp.arange(n_blocks * cb, dtype=jnp.int32)
    read_src = jnp.where(used, filled[:, 2] + c, spare0 + 2 * cb).astype(jnp.int32)
    write_dst = jnp.where(used, filled[:, 2] + c, spare0 + k % (2 * cb)).astype(jnp.int32)
    blk = jnp.arange(n_blocks, dtype=jnp.int32) * cb
    blk_e = jnp.minimum(jnp.sum((expert_end[None, :] <= blk[:, None]).astype(jnp.int32), axis=1), N_EXPERTS - 1)
    n_active = (expert_end[-1] // cb).astype(jnp.int32).reshape(1)
    return blk_e, n_active, read_src, write_dst


def _expert_kernel(blk_e_ref, nact_ref, rsrc_ref, wdst_ref, xs_hbm, wg_ref, wu_ref, wd_ref, out_hbm,
                   xbuf, obuf, gsem, ssem):
    del blk_e_ref
    b = pl.program_id(0)
    n_active = nact_ref[0]
    cb = CHUNKS_PER_BLOCK

    def chunk_copies(blk, slot, gather, start):
        for k in range(cb):
            local = pl.ds(k * SEG_ALIGN, SEG_ALIGN)
            if gather:
                ch = rsrc_ref[blk * cb + k]
                rows = pl.ds(pl.multiple_of(ch * SEG_ALIGN, SEG_ALIGN), SEG_ALIGN)
                cp = pltpu.make_async_copy(xs_hbm.at[rows], xbuf.at[slot, local], gsem.at[slot])
            else:
                ch = wdst_ref[blk * cb + k]
                rows = pl.ds(pl.multiple_of(ch * SEG_ALIGN, SEG_ALIGN), SEG_ALIGN)
                cp = pltpu.make_async_copy(obuf.at[slot, local], out_hbm.at[rows], ssem.at[slot])
            if start:
                cp.start()
            else:
                cp.wait()

    @pl.when(b == 0)
    def _():
        chunk_copies(0, 0, True, True)

    @pl.when(b < n_active)
    def _():
        slot = b % 2

        @pl.when(b + 1 < n_active)
        def _():
            chunk_copies(b + 1, 1 - slot, True, True)

        chunk_copies(b, slot, True, False)

        @pl.when(b >= 2)
        def _():
            chunk_copies(b - 2, slot, False, False)

        x = xbuf[slot]
        xb = _unpack_pairs(x[:, 0:PACK_W])
        w = pltpu.bitcast(x[:, PACK_W:ROW_W], F32)
        hid = _silu(_dot(xb, wg_ref[0])) * _dot(xb, wu_ref[0]) * jnp.concatenate([w, w], axis=1)
        obuf[slot, :, 0:PACK_W] = _pack_pairs(_dot(hid.astype(BF16), wd_ref[0]))
        obuf[slot, :, PACK_W:ROW_W] = x[:, PACK_W:ROW_W]
        chunk_copies(b, slot, False, True)

        @pl.when(b == n_active - 1)
        def _():
            chunk_copies(b, slot, False, False)

            @pl.when(b >= 1)
            def _():
                chunk_copies(b - 1, 1 - slot, False, False)


def _experts(xs, plan, wg, wu, wd):
    blk_e, n_active, read_src, write_dst = plan
    n_blocks = blk_e.shape[0]
    d = wg.shape[1]
    wspec = lambda shape: pl.BlockSpec((1,) + shape, lambda b, be, na, rs, wd_: (be[b], 0, 0))
    return pl.pallas_call(
        _expert_kernel,
        grid_spec=pltpu.PrefetchScalarGridSpec(
            num_scalar_prefetch=4,
            grid=(n_blocks,),
            in_specs=[pl.BlockSpec(memory_space=pl.ANY),
                      wspec((d, EXPERT_DIM)), wspec((d, EXPERT_DIM)), wspec((EXPERT_DIM, d))],
            out_specs=pl.BlockSpec(memory_space=pl.ANY),
            scratch_shapes=[pltpu.VMEM((2, EXPERT_BLOCK, ROW_W), xs.dtype),
                            pltpu.VMEM((2, EXPERT_BLOCK, ROW_W), xs.dtype),
                            pltpu.SemaphoreType.DMA((2,)), pltpu.SemaphoreType.DMA((2,))]),
        out_shape=jax.ShapeDtypeStruct(xs.shape, xs.dtype),
        input_output_aliases={4: 0},
        compiler_params=_cparams(("arbitrary",)),
        name="moe_experts",
    )(blk_e, n_active, read_src, write_dst, xs, wg, wu, wd)


def _combine_kernel(ys_ref, wtok_ref, h_ref, x1_ref, mod_ref, ln_ref, il_ref, su_ref,
                    sg_ref, su2_ref, sd_ref, o_ref, *, alpha):
    tm = h_ref.shape[0]
    r1 = ys_ref.shape[0]
    chosen = wtok_ref[...] > 0.0
    mask = jnp.where(chosen, 1.0, 0.0).astype(BF16)
    cnt = _dot(il_ref[...], mask)
    len_row = _ceil_seg(_dot(jnp.ones((8, tm), BF16), mask))
    off_row = _dot(len_row.astype(BF16), su_ref[...])
    dhi, dlo = _split_256(jnp.where(chosen, cnt + off_row[0:1, :] - 1.0, -1.0))
    len_col = _ceil_seg(_dot_tn(mask, jnp.ones((tm, EXPERT_LANES), BF16)))
    off_col = _dot_tn(su_ref[...], len_col.astype(BF16))
    reps = r1 // EXPERT_LANES
    lo = jnp.concatenate([off_col] * reps, axis=1)
    hi = lo + jnp.concatenate([len_col] * reps, axis=1)
    rows_e = lax.broadcasted_iota(jnp.int32, (EXPERT_LANES, r1), 1).astype(F32)
    segb = jnp.where(rows_e >= lo, jnp.where(rows_e < hi, 1.0, 0.0), 0.0).astype(BF16)
    want = 256.0 * _dot(dhi, segb) + _dot(dlo, segb)
    rows_t = lax.broadcasted_iota(jnp.int32, (tm, r1), 1).astype(F32)
    onehot = jnp.where(want == rows_t, 1.0, 0.0).astype(BF16)
    y = _dot(onehot, _unpack_pairs(ys_ref[:, 0:PACK_W]))
    x = h_ref[...]
    y = y + _dot((_silu(_dot(x, sg_ref[...])) * _dot(x, su2_ref[...])).astype(BF16), sd_ref[...])
    gate2 = mod_ref[0, 0, 5:6, :]
    o_ref[...] = _ln(alpha * x1_ref[...] + gate2 * y) * ln_ref[0:1, :] + ln_ref[1:2, :]


def _combine(ys, w_tok, h2, x1, mods, ln2, sg, su, sd, alpha, tiles_per_seq, n_ctx_tiles):
    n, d = h2.shape
    tiles = n // ROW_TILE
    iu, sup = _moe_tables(ROW_TILE)
    full = lambda arr: pl.BlockSpec(arr.shape, lambda i: (0,) * arr.ndim)
    row = pl.BlockSpec((ROW_TILE, d), lambda i: (i, 0))
    il = jnp.asarray(iu.T, BF16)
    sup = jnp.asarray(sup, BF16)
    return pl.pallas_call(
        functools.partial(_combine_kernel, alpha=alpha),
        grid=(tiles,),
        in_specs=[pl.BlockSpec((TILE_ROWS, ROW_W), lambda i: (i, 0)),
                  pl.BlockSpec((ROW_TILE, EXPERT_LANES), lambda i: (i, 0)),
                  row, row,
                  pl.BlockSpec((1, 1, 6, d), lambda i: (i // tiles_per_seq,
                                                        (i % tiles_per_seq >= n_ctx_tiles).astype(jnp.int32), 0, 0)),
                  full(ln2), full(il), full(sup), full(sg), full(su), full(sd)],
        out_specs=row,
        out_shape=jax.ShapeDtypeStruct((n, d), F32),
        compiler_params=_cparams(("parallel",)),
        name="moe_combine",
    )(ys, w_tok, h2, x1, mods, ln2, il, sup, sg, su, sd)


def _rope_tables(n_lat, n_ctx):
    rows = n_lat // GRID_W
    row = jnp.repeat(jnp.arange(rows, dtype=F32), GRID_W)
    col = (jnp.arange(n_lat) % GRID_W).astype(F32)
    n_freq = DA_DH // 4
    inv = ROPE_THETA ** (-jnp.arange(n_freq, dtype=F32) / n_freq)
    ang = jnp.concatenate([row[:, None] * inv, col[:, None] * inv], axis=-1)
    ang = jnp.concatenate([ang, ang], axis=-1)
    cos, sin = jnp.cos(ang), jnp.sin(ang)
    sign = jnp.where(jnp.arange(DA_DH) < DA_DH // 2, -1.0, 1.0).astype(F32)
    cos = jnp.tile(cos, (1, HEAD_W // DA_DH))
    sin = jnp.tile(sin * sign, (1, HEAD_W // DA_DH))
    cos = jnp.concatenate([jnp.ones((n_ctx, HEAD_W), F32), cos], axis=0)
    sin = jnp.concatenate([jnp.zeros((n_ctx, HEAD_W), F32), sin], axis=0)
    return cos, sin


def kernel(x, c, ctx, c_ctx, w_ada, b_ada, w_in, lb_logits, hg_norm_g, lam_params, da_norm_g, w_branch_a, w_branch_b, w_branch_c, w_out, ln1_g, ln1_b, ln2_g, ln2_b, w_router, router_bias, w_exp_gate, w_exp_up, w_exp_down, w_sh_gate, w_sh_up, w_sh_down):
    bsz, n_lat, d = x.shape
    n_ctx = ctx.shape[1]
    depth = w_in.shape[0]
    t = n_ctx + n_lat
    assert n_ctx % ROW_TILE == 0 and n_lat % ROW_TILE == 0
    assert n_ctx % HG_CHUNK == 0 and n_lat % HG_CHUNK == 0 and n_lat % GRID_W == 0
    n_ctx_tiles = n_ctx // ROW_TILE
    alpha = (2 * depth) ** 0.25
    per = N_EXPERTS // N_GROUPS

    cos, sin = _rope_tables(n_lat, n_ctx)
    sm = jax.nn.softmax(lb_logits.astype(F32), axis=0)
    lower = jnp.cumsum(sm, axis=0) - sm[0:1]
    log_lb = jnp.log(jnp.maximum(lower, LB_FLOOR))
    log1m_lb = jnp.log1p(-lower)

    n_rows = -(-(bsz + 1) // 8) * 8
    cvec = jnp.concatenate([c, c_ctx[None, :], jnp.zeros((n_rows - bsz - 1, d), F32)], axis=0)

    n_branch_cols = w_in.shape[2] - 3 * d

    def member_major(a):
        return a.reshape((N_GROUPS, per) + a.shape[1:]).swapaxes(0, 1).reshape(a.shape)

    xs = jnp.concatenate([ctx, x], axis=1)
    for l in range(depth):
        mod = _ada(cvec, w_ada[l], b_ada[l])
        mod_lat = mod[:bsz].reshape(bsz, 1, 6, d)
        mod_ctx = jnp.broadcast_to(mod[bsz].reshape(1, 1, 6, d), (bsz, 1, 6, d))
        mods = jnp.concatenate([mod_ctx, mod_lat], axis=1)

        w_in_l = jnp.concatenate([w_in[l][:, n_branch_cols:], w_in[l][:, :n_branch_cols]], axis=1)
        p = _inproj(xs, mods, w_in_l.astype(BF16), n_ctx)
        lb_f = jnp.stack([log_lb[l, 0], log1m_lb[l, 0]])
        lb_b = jnp.stack([log_lb[l, 1], log1m_lb[l, 1]])
        o_f, o_b = _hgrn(p, lb_f, lb_b, n_ctx)

        lam_init = 0.8 - 0.6 * math.exp(-0.3 * l)
        lp = lam_params[l].astype(F32)
        lam = jnp.exp(jnp.sum(lp[0] * lp[1])) - jnp.exp(jnp.sum(lp[2] * lp[3])) + lam_init
        qr, kr, vr = _attn_prep(p, cos, sin)
        at = _attention(qr, kr, vr, jnp.full((8, HEAD_W), lam, F32), n_ctx)
        fr = _fourier(p, n_ctx)

        skip = n_ctx_tiles if l == depth - 1 else 0
        ln1 = jnp.stack([ln1_g[l], ln1_b[l]])
        hg_g = hg_norm_g[l].reshape(1, HEAD_W)
        da_g = da_norm_g[l].reshape(1, HEAD_W)
        x1, h2, logits_t = _merge(
            xs, mods, p, o_f, o_b, at, fr, hg_g, da_g,
            w_branch_a[l].astype(BF16), w_branch_b[l].astype(BF16), w_branch_c[l].astype(BF16),
            w_out[l].astype(BF16), ln1, member_major(w_router[l].T), alpha, 1.0 - lam_init,
            n_ctx_tiles, skip)

        bias_col = jnp.broadcast_to(member_major(router_bias[l])[:, None], (N_EXPERTS, ROUTER_TILE))
        wts_t = _router(logits_t, bias_col)
        t_out = x1.shape[1]
        n_tok = bsz * t_out
        w_tok = jnp.concatenate([wts_t.T, jnp.zeros((n_tok, EXPERT_LANES - N_EXPERTS), F32)], axis=1)
        h2f = h2.reshape(n_tok, d)
        rows = _dispatch(h2f, wts_t, w_tok)
        plan = _moe_plan(wts_t, n_tok // ROW_TILE)
        rows = _experts(rows, plan, member_major(w_exp_gate[l]).astype(BF16),
                        member_major(w_exp_up[l]).astype(BF16), member_major(w_exp_down[l]).astype(BF16))
        ln2 = jnp.stack([ln2_g[l], ln2_b[l]])
        xs = _combine(rows, w_tok, h2f, x1.reshape(n_tok, d), mods, ln2,
                      w_sh_gate[l].astype(BF16), w_sh_up[l].astype(BF16), w_sh_down[l].astype(BF16),
                      alpha, t_out // ROW_TILE, n_ctx_tiles - skip).reshape(bsz, t_out, d)
    return xs if xs.shape[1] == n_lat else xs[:, n_ctx:]
```

```python
import functools
import math

import numpy as np
import jax
import jax.numpy as jnp
from jax import lax
from jax.experimental import pallas as pl
from jax.experimental.pallas import tpu as pltpu

F32 = jnp.float32
BF16 = jnp.bfloat16

HG_HEADS = 4
HG_DK = 128
LB_FLOOR = 1e-30
DA_HEADS = 4
DA_DH = 64
ROPE_THETA = 10000.0
GRID_W = 64
FR_GROUPS = 4
FR_DG = 128
N_EXPERTS = 64
TOP_K = 8
EXPERT_DIM = 256
N_GROUPS = 8
TOPK_GROUPS = 4
ROUTED_SCALE = 2.5
MASK_VALUE = -1e9
LN_EPS = 1e-5
LOG2_E = 1.4426950408889634

HEAD_W = 128
BRANCH_W = 512
GATES_BLOCKS = 6
COL_Q, COL_FF, COL_FB, COL_I, COL_OG, COL_QA, COL_KA, COL_VA, COL_U = range(6, 15)

ROW_TILE = 256
HG_CHUNK = 128
HG_LEVELS = (64, 32, 16, 8, 4, 2, 1)
HG_ROW_REF_MIN = 8
IN_N_TILE = 2560
IN_ROW_TILE = 768
ROUTER_TILE = 512
VMEM_LIMIT = 48 * 1024 * 1024


def _cparams(sem):
    return pltpu.CompilerParams(dimension_semantics=sem, vmem_limit_bytes=VMEM_LIMIT)


def _dot(a, b):
    return jnp.dot(a, b, preferred_element_type=F32)


def _dot_nt(a, b):
    return lax.dot_general(a, b, (((1,), (1,)), ((), ())), preferred_element_type=F32)


def _dot_tn(a, b):
    return lax.dot_general(a, b, (((0,), (0,)), ((), ())), preferred_element_type=F32)


def _silu(x):
    return x * jax.nn.sigmoid(x)


def _ln(x):
    mu = jnp.mean(x, axis=-1, keepdims=True)
    xc = x - mu
    var = jnp.mean(xc * xc, axis=-1, keepdims=True)
    return xc * lax.rsqrt(var + LN_EPS)


def _ada_kernel(c_ref, w_ref, b_ref, o_ref):
    o_ref[...] = jnp.dot(_silu(c_ref[...]), w_ref[...], preferred_element_type=F32,
                         precision=lax.Precision.HIGHEST) + b_ref[...]


def _ada(cvec, w, b):
    rows, d = cvec.shape
    n = w.shape[1]
    tn = 1536
    return pl.pallas_call(
        _ada_kernel,
        grid=(n // tn,),
        in_specs=[pl.BlockSpec((rows, d), lambda j: (0, 0)),
                  pl.BlockSpec((d, tn), lambda j: (0, j)),
                  pl.BlockSpec((1, tn), lambda j: (0, j))],
        out_specs=pl.BlockSpec((rows, tn), lambda j: (0, j)),
        out_shape=jax.ShapeDtypeStruct((rows, n), F32),
        compiler_params=_cparams(("parallel",)),
        name="ada",
    )(cvec, w, b.reshape(1, n))


def _inproj_kernel(x_ref, mod_ref, w_ref, o_ref, *, n_ctx):
    first = pl.program_id(2) == 0
    xn = _ln(x_ref[0])
    shift_top = jnp.where(first, mod_ref[0, 0, 0:1, :], mod_ref[0, 1, 0:1, :])
    scale_top = jnp.where(first, mod_ref[0, 0, 1:2, :], mod_ref[0, 1, 1:2, :])
    parts = [xn[:n_ctx] * (1.0 + scale_top) + shift_top]
    if xn.shape[0] > n_ctx:
        parts.append(xn[n_ctx:] * (1.0 + mod_ref[0, 1, 1:2, :]) + mod_ref[0, 1, 0:1, :])
    h = jnp.concatenate(parts, axis=0) if len(parts) > 1 else parts[0]
    o_ref[0] = _dot(h.astype(BF16), w_ref[...])


def _inproj(x, mods, w, n_ctx):
    bsz, t, d = x.shape
    n = w.shape[1]
    tm = IN_ROW_TILE if t % IN_ROW_TILE == 0 and n_ctx <= IN_ROW_TILE else n_ctx
    return pl.pallas_call(
        functools.partial(_inproj_kernel, n_ctx=n_ctx),
        grid=(n // IN_N_TILE, bsz, t // tm),
        in_specs=[pl.BlockSpec((1, tm, d), lambda j, b, i: (b, i, 0)),
                  pl.BlockSpec((1, 2, 6, d), lambda j, b, i: (b, 0, 0, 0)),
                  pl.BlockSpec((d, IN_N_TILE), lambda j, b, i: (0, j))],
        out_specs=pl.BlockSpec((1, tm, IN_N_TILE), lambda j, b, i: (b, i, j)),
        out_shape=jax.ShapeDtypeStruct((bsz, t, n), F32),
        compiler_params=_cparams(("parallel", "parallel", "parallel")),
        name="inproj",
    )(x, mods, w)


def _hgrn_tables(reverse):
    c = HG_CHUNK
    t = np.arange(c)
    if reverse:
        cum = (t[None, :] >= t[:, None])
    else:
        cum = (t[None, :] <= t[:, None])
    mats = [cum]
    for m in HG_LEVELS:
        if m < HG_ROW_REF_MIN:
            idx = (t // (2 * m)) * (2 * m) + m - (1 if reverse else 0)
            mats.append(cum[idx])
    stack = np.concatenate(mats, axis=0).astype(np.float32)
    x = t[:, None] ^ t[None, :]
    lvl = np.zeros((c, c), np.int32)
    for li, m in enumerate(HG_LEVELS):
        lvl = np.where((x >= m) & (x < 2 * m), li + 1, lvl)
    visible = (t[None, :] >= t[:, None]) if reverse else (t[None, :] <= t[:, None])
    lvl = np.where(visible, lvl, -1).astype(np.int32)
    return stack, lvl


def _hgrn_kernel(qf_ref, ff_ref, vf_ref, qb_ref, fb_ref, vb_ref, lbf_ref, lbb_ref,
                 stf_ref, lvf_ref, stb_ref, lvb_ref, of_ref, ob_ref, sf_ref, sb_ref):
    c = HG_CHUNK

    @pl.when(pl.program_id(1) == 0)
    def _():
        sf_ref[...] = jnp.zeros_like(sf_ref)
        sb_ref[...] = jnp.zeros_like(sb_ref)

    chains = []
    for h in range(HG_HEADS):
        chains.append((qf_ref, ff_ref, vf_ref, lbf_ref, stf_ref, lvf_ref, of_ref, sf_ref, h, False))
        chains.append((qb_ref, fb_ref, vb_ref, lbb_ref, stb_ref, lvb_ref, ob_ref, sb_ref, h, True))

    gates = []
    for q_ref, f_ref, v_ref, lb_ref, _, _, _, _, h, _ in chains:
        sl = slice(h * HEAD_W, (h + 1) * HEAD_W)
        q = _silu(q_ref[0, :, sl])
        fr = f_ref[0, :, sl]
        v = v_ref[0, :, sl].astype(BF16)
        log_lb = lb_ref[0:1, sl]
        log1m_lb = lb_ref[1:2, sl]
        c2 = log1m_lb + jnp.minimum(fr, 0.0) - jnp.log1p(jnp.exp(-jnp.abs(fr)))
        g = jnp.maximum(log_lb, c2) + jnp.log1p(jnp.exp(-jnp.abs(log_lb - c2)))
        gates.append((q, 1.0 - jnp.exp(g), v, g))

    cums = []
    for (_, _, _, _, stack_ref, _, _, _, _, _), (_, _, _, g) in zip(chains, gates):
        g1 = g.astype(BF16)
        r1 = g - g1.astype(F32)
        g2 = r1.astype(BF16)
        g3 = (r1 - g2.astype(F32)).astype(BF16)
        stack = stack_ref[...]
        cums.append(_dot(stack, g1) + _dot(stack, g2) + _dot(stack, g3))

    acc = []
    qk16 = [(q.astype(BF16), k.astype(BF16)) for q, k, _, _ in gates]
    for (_, _, _, _, _, lvl_ref, _, _, _, _), (q16, k16) in zip(chains, qk16):
        acc.append(jnp.where(lvl_ref[...] == 0, _dot_nt(q16, k16), 0.0))
    n_stacked = 0
    for li, m in enumerate(HG_LEVELS):
        if m < HG_ROW_REF_MIN:
            n_stacked += 1
        for ci, (chain, (q16, k16), bb) in enumerate(zip(chains, qk16, cums)):
            reverse = chain[9]
            b = bb[0:c]
            if m >= HG_ROW_REF_MIN:
                rows = []
                for j in range(c // (2 * m)):
                    idx = j * 2 * m + m - (1 if reverse else 0)
                    rows.append(jnp.broadcast_to(b[idx:idx + 1, :], (2 * m, HEAD_W)))
                ref = rows[0] if len(rows) == 1 else jnp.concatenate(rows, axis=0)
            else:
                ref = bb[n_stacked * c:(n_stacked + 1) * c]
            e = jnp.exp(-jnp.abs(b - ref)).astype(BF16)
            acc[ci] = jnp.where(chain[5][...] == li + 1, _dot_nt(q16 * e, k16 * e), acc[ci])

    for chain, (q, k, v, _), bb, a in zip(chains, gates, cums, acc):
        o_ref, st_ref, h, reverse = chain[6], chain[7], chain[8], chain[9]
        sl = slice(h * HEAD_W, (h + 1) * HEAD_W)
        b = bb[0:c]
        st = st_ref[h]
        o_ref[0, :, sl] = _dot(a.astype(BF16), v) + _dot_nt((q * jnp.exp(b)).astype(BF16), st.astype(BF16))
        b_end = b[0:1] if reverse else b[c - 1:c]
        k_end = (k * jnp.exp(b_end - b)).astype(BF16)
        st_ref[h] = st * jnp.exp(b_end) + _dot_tn(v, k_end)


def _hgrn(p, lb_f, lb_b, n_ctx):
    bsz, t, _ = p.shape
    c = HG_CHUNK
    nc, nt = n_ctx // c, t // c
    stack_f, lvl_f = _hgrn_tables(False)
    stack_b, lvl_b = _hgrn_tables(True)

    def rev(j):
        return jnp.where(j < nc, nc - 1 - j, nc + nt - 1 - j)

    nst = stack_f.shape[0]
    blk = lambda row, col: pl.BlockSpec((1, c, BRANCH_W), lambda b, j: (b, row(j), col))
    const = lambda shape: pl.BlockSpec(shape, lambda b, j: (0, 0))
    fwd = lambda j: j
    out = jax.ShapeDtypeStruct((bsz, t, BRANCH_W), F32)
    state = pltpu.VMEM((HG_HEADS, HEAD_W, HEAD_W), F32)
    return pl.pallas_call(
        _hgrn_kernel,
        grid=(bsz, nt),
        in_specs=[blk(fwd, COL_Q), blk(fwd, COL_FF), blk(fwd, COL_I),
                  blk(rev, COL_Q), blk(rev, COL_FB), blk(rev, COL_I),
                  const((2, BRANCH_W)), const((2, BRANCH_W)),
                  const((nst, c)), const((c, c)), const((nst, c)), const((c, c))],
        out_specs=[blk(fwd, 0), blk(rev, 0)],
        out_shape=[out, out],
        scratch_shapes=[state, state],
        compiler_params=_cparams(("parallel", "arbitrary")),
        name="hgrn",
    )(p, p, p, p, p, p, lb_f, lb_b, jnp.asarray(stack_f, BF16), jnp.asarray(lvl_f),
      jnp.asarray(stack_b, BF16), jnp.asarray(lvl_b))


def _attn_prep_kernel(q_ref, k_ref, v_ref, cos_ref, sin_ref, qo_ref, ko_ref, vo_ref):
    cos = cos_ref[...]
    sin = sin_ref[...]
    lane = lax.broadcasted_iota(jnp.int32, cos.shape, 1)
    first_half = (lane % DA_DH) < (DA_DH // 2)

    def rope(x):
        up = pltpu.roll(x, HEAD_W - DA_DH // 2, 1)
        down = pltpu.roll(x, DA_DH // 2, 1)
        return x * cos + jnp.where(first_half, up, down) * sin

    ones = jnp.ones((q_ref.shape[1], HEAD_W), BF16)
    for h in range(DA_HEADS):
        sl = slice(h * HEAD_W, (h + 1) * HEAD_W)
        qo_ref[0, :, sl] = (rope(q_ref[0, :, sl]) * (DA_DH ** -0.5 * LOG2_E)).astype(BF16)
        ko_ref[0, :, sl] = rope(k_ref[0, :, sl]).astype(BF16)
        vo_ref[0, :, 2 * h * HEAD_W:(2 * h + 1) * HEAD_W] = v_ref[0, :, sl].astype(BF16)
        vo_ref[0, :, (2 * h + 1) * HEAD_W:(2 * h + 2) * HEAD_W] = ones


def _attn_prep(p, cos, sin):
    bsz, t, _ = p.shape
    spec = lambda col: pl.BlockSpec((1, ROW_TILE, BRANCH_W), lambda b, i: (b, i, col))
    tab = pl.BlockSpec((ROW_TILE, HEAD_W), lambda b, i: (i, 0))
    out = jax.ShapeDtypeStruct((bsz, t, BRANCH_W), BF16)
    return pl.pallas_call(
        _attn_prep_kernel,
        grid=(bsz, t // ROW_TILE),
        in_specs=[spec(COL_QA), spec(COL_KA), spec(COL_VA), tab, tab],
        out_specs=[spec(0), spec(0), pl.BlockSpec((1, ROW_TILE, 2 * BRANCH_W), lambda b, i: (b, i, 0))],
        out_shape=[out, out, jax.ShapeDtypeStruct((bsz, t, 2 * BRANCH_W), BF16)],
        compiler_params=_cparams(("parallel", "parallel")),
        name="attn_prep",
    )(p, p, p, cos, sin)


def _attn_kernel(q_ref, k_ref, v_ref, lam_ref, o_ref, *, n_ctx, n_ctx_tiles):
    lam = lam_ref[0:1, 0:1]

    def attend(n_keys):
        k = k_ref[0, 0:n_keys, :]
        v = v_ref[0, 0:n_keys, :]
        half = q_ref.shape[1] // 2
        lane = lax.broadcasted_iota(jnp.int32, (half, HEAD_W), 1)
        chains = [(r, first) for r in range(2) for first in (True, False)]
        scores = []
        for r, first in chains:
            q = q_ref[0, r * half:(r + 1) * half, :]
            qm = jnp.where((lane < DA_DH) == first, q, jnp.zeros_like(q))
            scores.append(_dot_nt(qm, k))
        probs = [jnp.exp2(s - jnp.max(s, axis=-1, keepdims=True)).astype(BF16) for s in scores]
        outs = []
        for e in probs:
            pv = _dot(e, v)
            outs.append(pv[:, 0:HEAD_W] / pv[:, HEAD_W:2 * HEAD_W])
        for r in range(2):
            o_ref[0, r * half:(r + 1) * half, :] = outs[2 * r] - lam * outs[2 * r + 1]

    is_ctx = pl.program_id(2) < n_ctx_tiles

    @pl.when(is_ctx)
    def _():
        attend(n_ctx)

    @pl.when(jnp.logical_not(is_ctx))
    def _():
        attend(k_ref.shape[1])


def _attention(q, k, v, lam, n_ctx):
    bsz, t, _ = q.shape
    n_ctx_tiles = n_ctx // ROW_TILE
    return pl.pallas_call(
        functools.partial(_attn_kernel, n_ctx=n_ctx, n_ctx_tiles=n_ctx_tiles),
        grid=(bsz, DA_HEADS, t // ROW_TILE),
        in_specs=[pl.BlockSpec((1, ROW_TILE, HEAD_W), lambda b, h, i: (b, i, h)),
                  pl.BlockSpec((1, t, HEAD_W), lambda b, h, i: (b, 0, h)),
                  pl.BlockSpec((1, t, 2 * HEAD_W), lambda b, h, i: (b, 0, h)),
                  pl.BlockSpec((8, HEAD_W), lambda b, h, i: (0, 0))],
        out_specs=pl.BlockSpec((1, ROW_TILE, HEAD_W), lambda b, h, i: (b, i, h)),
        out_shape=jax.ShapeDtypeStruct((bsz, t, BRANCH_W), F32),
        compiler_params=_cparams(("parallel", "parallel", "parallel")),
        name="diff_attn",
    )(q, k, v, lam)


def _dft_tables(n):
    idx = np.arange(n, dtype=np.int64)
    ang = 2.0 * np.pi * ((idx[:, None] * idx[None, :]) % n).astype(np.float64) / n
    scale = 1.0 / math.sqrt(n * FR_DG)
    return np.concatenate([np.cos(ang), -np.sin(ang)], axis=1).astype(np.float32) * np.float32(scale)


def _channel_dft_tables():
    idx = np.arange(FR_DG, dtype=np.int64)
    ang = 2.0 * np.pi * ((idx[:, None] * idx[None, :]) % FR_DG).astype(np.float64) / FR_DG
    eye = np.eye(FR_GROUPS)
    return (np.kron(eye, np.cos(ang)).astype(np.float32), np.kron(eye, np.sin(ang)).astype(np.float32))


def _fourier_kernel(u_ref, cc_ref, sc_ref, wc_ref, wl_ref, o_ref, zc_ref, zl_ref, *, n_ctx, n_ctx_tiles):
    i = pl.program_id(1)
    n_lat = u_ref.shape[1] - n_ctx

    @pl.when(i == 0)
    def _():
        u = u_ref[0].astype(BF16)
        zcos = _dot(u, cc_ref[...]).astype(BF16)
        zsin = _dot(u, sc_ref[...]).astype(BF16)
        zc_ref[0:n_ctx] = zcos[0:n_ctx]
        zc_ref[n_ctx:2 * n_ctx] = zsin[0:n_ctx]
        zl_ref[0:n_lat] = zcos[n_ctx:]
        zl_ref[n_lat:2 * n_lat] = zsin[n_ctx:]

    @pl.when(i < n_ctx_tiles)
    def _():
        o_ref[0] = _dot(wc_ref[...], zc_ref[...])

    @pl.when(i >= n_ctx_tiles)
    def _():
        o_ref[0] = _dot(wl_ref[...], zl_ref[...])


def _fourier(p, n_ctx):
    bsz, t, _ = p.shape
    n_lat = t - n_ctx
    n_ctx_tiles = n_ctx // ROW_TILE
    cc, sc = _channel_dft_tables()
    wc = jnp.asarray(_dft_tables(n_ctx)).astype(BF16)
    wl = jnp.asarray(_dft_tables(n_lat)).astype(BF16)
    cc = jnp.asarray(cc).astype(BF16)
    sc = jnp.asarray(sc).astype(BF16)
    return pl.pallas_call(
        functools.partial(_fourier_kernel, n_ctx=n_ctx, n_ctx_tiles=n_ctx_tiles),
        grid=(bsz, t // ROW_TILE),
        in_specs=[pl.BlockSpec((1, t, BRANCH_W), lambda b, i: (b, 0, COL_U)),
                  pl.BlockSpec((BRANCH_W, BRANCH_W), lambda b, i: (0, 0)),
                  pl.BlockSpec((BRANCH_W, BRANCH_W), lambda b, i: (0, 0)),
                  pl.BlockSpec((ROW_TILE, 2 * n_ctx), lambda b, i: (jnp.minimum(i, n_ctx_tiles - 1), 0)),
                  pl.BlockSpec((ROW_TILE, 2 * n_lat), lambda b, i: (jnp.maximum(i - n_ctx_tiles, 0), 0))],
        out_specs=pl.BlockSpec((1, ROW_TILE, BRANCH_W), lambda b, i: (b, i, 0)),
        out_shape=jax.ShapeDtypeStruct((bsz, t, BRANCH_W), F32),
        scratch_shapes=[pltpu.VMEM((2 * n_ctx, BRANCH_W), BF16), pltpu.VMEM((2 * n_lat, BRANCH_W), BF16)],
        compiler_params=_cparams(("parallel", "arbitrary")),
        name="fourier",
    )(p, cc, sc, wc, wl)


def _head_rms(x, gain):
    parts = []
    for h in range(BRANCH_W // HEAD_W):
        xh = x[:, h * HEAD_W:(h + 1) * HEAD_W]
        parts.append(xh * lax.rsqrt(jnp.mean(xh * xh, axis=-1, keepdims=True) + LN_EPS) * gain)
    return jnp.concatenate(parts, axis=-1)


def _merge_kernel(x_ref, mod_ref, gates_ref, og_ref, of_ref, ob_ref, at_ref, fr_ref,
                  hg_ref, da_ref, wa_ref, wb_ref, wc_ref, wo_ref, ln_ref, wr_ref,
                  x1_ref, h2_ref, lg_ref, *, alpha, attn_scale):
    d = x_ref.shape[2]
    tm = x_ref.shape[1]
    gate1 = mod_ref[0, 0, 2:3, :]
    shift2 = mod_ref[0, 0, 3:4, :]
    scale2 = mod_ref[0, 0, 4:5, :]
    halves = [slice(r * (tm // 2), (r + 1) * (tm // 2)) for r in range(2)]
    a = [(_head_rms(of_ref[0, rs, :] + ob_ref[0, rs, :], hg_ref[...]) * _silu(og_ref[0, rs, :])).astype(BF16)
         for rs in halves]
    bq = [(_head_rms(at_ref[0, rs, :], da_ref[...]) * attn_scale).astype(BF16) for rs in halves]
    pa = [_dot(v, wa_ref[...]) for v in a]
    pb = [_dot(v, wb_ref[...]) for v in bq]
    pc = [_dot(fr_ref[0, rs, :].astype(BF16), wc_ref[...]) for rs in halves]
    m = []
    for rs, ya, yb, yc in zip(halves, pa, pb, pc):
        m.append((jax.nn.sigmoid(gates_ref[0, rs, 0:d]) * ya + jax.nn.sigmoid(gates_ref[0, rs, d:2 * d]) * yb
                  + jax.nn.sigmoid(gates_ref[0, rs, 2 * d:3 * d]) * yc).astype(BF16))
    mix = [_dot(v, wo_ref[...]) for v in m]
    h2s = []
    for rs, mx in zip(halves, mix):
        x1 = _ln(alpha * x_ref[0, rs, :] + gate1 * mx) * ln_ref[0:1, :] + ln_ref[1:2, :]
        x1_ref[0, rs, :] = x1
        h2 = _ln(x1) * (1.0 + scale2) + shift2
        h2_ref[0, rs, :] = h2.astype(BF16)
        h2s.append(h2)
    for rs, h2 in zip(halves, h2s):
        lg_ref[:, rs] = lax.dot_general(wr_ref[...], h2, (((1,), (1,)), ((), ())),
                                        preferred_element_type=F32, precision=lax.Precision.HIGHEST)


def _merge(x, mods, p, o_f, o_b, at, fr, hg_g, da_g, wa, wb, wc, wo, ln1, wr_t, alpha, attn_scale,
           n_ctx_tiles, skip_tiles):
    bsz, t, d = x.shape
    tiles = t // ROW_TILE - skip_tiles
    t_out = tiles * ROW_TILE
    row = lambda w, col: pl.BlockSpec((1, ROW_TILE, w), lambda b, i: (b, i + skip_tiles, col))
    out_row = pl.BlockSpec((1, ROW_TILE, d), lambda b, i: (b, i, 0))
    full = lambda arr: pl.BlockSpec(arr.shape, lambda b, i: (0,) * arr.ndim)
    return pl.pallas_call(
        functools.partial(_merge_kernel, alpha=alpha, attn_scale=attn_scale),
        grid=(bsz, tiles),
        in_specs=[row(d, 0),
                  pl.BlockSpec((1, 1, 6, d),
                               lambda b, i: (b, (i + skip_tiles >= n_ctx_tiles).astype(jnp.int32), 0, 0)),
                  row(GATES_BLOCKS * BRANCH_W, 0), row(BRANCH_W, COL_OG),
                  row(BRANCH_W, 0), row(BRANCH_W, 0), row(BRANCH_W, 0), row(BRANCH_W, 0),
                  full(hg_g), full(da_g), full(wa), full(wb), full(wc), full(wo), full(ln1), full(wr_t)],
        out_specs=[out_row, out_row,
                   pl.BlockSpec((N_EXPERTS, ROW_TILE), lambda b, i: (0, b * tiles + i))],
        out_shape=[jax.ShapeDtypeStruct((bsz, t_out, d), F32),
                   jax.ShapeDtypeStruct((bsz, t_out, d), BF16),
                   jax.ShapeDtypeStruct((N_EXPERTS, bsz * t_out), F32)],
        compiler_params=_cparams(("parallel", "parallel")),
        name="merge",
    )(x, mods, p, p, o_f, o_b, at, fr, hg_g, da_g, wa, wb, wc, wo, ln1, wr_t)


def _router_kernel(lg_ref, bias_ref, w_ref):
    per = N_EXPERTS // N_GROUPS
    scores = [jax.nn.sigmoid(lg_ref[j * N_GROUPS:(j + 1) * N_GROUPS, :]) for j in range(per)]
    sel = [scores[j] + bias_ref[j * N_GROUPS:(j + 1) * N_GROUPS, :] for j in range(per)]
    shape = sel[0].shape
    grp = lax.broadcasted_iota(jnp.int32, shape, 0)

    def beats(other, me, tie):
        if isinstance(tie, bool):
            return jnp.where((other >= me) if tie else (other > me), 1, 0)
        return jnp.where(other > me, 1, jnp.where(other == me, tie, 0))

    m1 = jnp.maximum(sel[0], sel[1])
    m2 = jnp.minimum(sel[0], sel[1])
    for j in range(2, per):
        m2 = jnp.maximum(m2, jnp.minimum(m1, sel[j]))
        m1 = jnp.maximum(m1, sel[j])
    gscore = m1 + m2
    grank = jnp.zeros(shape, jnp.int32)
    lower_group = [None] + [jnp.where(grp >= kk, 1, 0) for kk in range(1, N_GROUPS)]
    for kk in range(1, N_GROUPS):
        other = pltpu.roll(gscore, kk, 0)
        grank = grank + beats(other, gscore, lower_group[kk])
    gmask = grank < TOPK_GROUPS
    sel = [jnp.where(gmask, s, MASK_VALUE) for s in sel]
    rolled = [[None] + [pltpu.roll(s, kk, 0) for kk in range(1, N_GROUPS)] for s in sel]
    weights = []
    for j in range(per):
        rank = jnp.zeros(shape, jnp.int32)
        for j2 in range(per):
            for kk in range(N_GROUPS):
                if kk == 0 and j2 == j:
                    continue
                if kk == 0:
                    rank = rank + beats(sel[j2], sel[j], j2 < j)
                else:
                    rank = rank + beats(rolled[j2][kk], sel[j], lower_group[kk])
        weights.append(jnp.where(rank < TOP_K, scores[j], 0.0))
    total = weights[0]
    for j in range(1, per):
        total = total + weights[j]
    total = jnp.sum(total, axis=0, keepdims=True)
    for j in range(per):
        w_ref[j * N_GROUPS:(j + 1) * N_GROUPS, :] = weights[j] / total * ROUTED_SCALE


def _router(logits_t, bias_col):
    e, n = logits_t.shape
    return pl.pallas_call(
        _router_kernel,
        grid=(n // ROUTER_TILE,),
        in_specs=[pl.BlockSpec((e, ROUTER_TILE), lambda i: (0, i)),
                  pl.BlockSpec((e, ROUTER_TILE), lambda i: (0, 0))],
        out_specs=pl.BlockSpec((e, ROUTER_TILE), lambda i: (0, i)),
        out_shape=jax.ShapeDtypeStruct((e, n), F32),
        compiler_params=_cparams(("parallel",)),
        name="router",
    )(logits_t, bias_col)


SEG_ALIGN = 8
TILE_ROWS = ROW_TILE * TOP_K + N_EXPERTS * SEG_ALIGN
CHUNKS_PER_BLOCK = 64
EXPERT_BLOCK = CHUNKS_PER_BLOCK * SEG_ALIGN
PACK_W = 512
ROW_W = PACK_W + HEAD_W
EXPERT_LANES = 128
HI_MASK = -65536


def _pack_pairs(x):
    hi = pltpu.bitcast(x[:, :PACK_W].astype(jnp.bfloat16).astype(F32), jnp.int32)
    lo = pltpu.bitcast(x[:, PACK_W:].astype(jnp.bfloat16).astype(F32), jnp.int32)
    return hi | lax.shift_right_logical(lo, jnp.full_like(lo, 16))


def _pack_bf16_valued_pairs(x):
    hi = pltpu.bitcast(x[:, :PACK_W], jnp.int32) & HI_MASK
    lo = pltpu.bitcast(x[:, PACK_W:], jnp.int32)
    return hi | lax.shift_right_logical(lo, jnp.full_like(lo, 16))


def _unpack_pairs(u):
    hi = pltpu.bitcast(u & HI_MASK, F32)
    lo = pltpu.bitcast(lax.shift_left(u, jnp.full_like(u, 16)), F32)
    return jnp.concatenate([hi, lo], axis=1).astype(BF16)


def _ceil_seg(c):
    return jnp.floor((c + (SEG_ALIGN - 1.0)) * (1.0 / SEG_ALIGN)) * SEG_ALIGN


def _split_256(c):
    hi = jnp.floor(c * (1.0 / 256.0))
    return hi.astype(BF16), (c - 256.0 * hi).astype(BF16)


def _moe_tables(tm):
    t = np.arange(tm)
    e = np.arange(EXPERT_LANES)
    incl_upper = (t[:, None] <= t[None, :]).astype(np.float32)
    strict_upper = (e[:, None] < e[None, :]).astype(np.float32)
    return incl_upper, strict_upper


def _dispatch_kernel(h_ref, wt_ref, wtok_ref, iu_ref, su_ref, o_ref):
    last = pl.num_programs(0) - 1

    @pl.when(pl.program_id(0) == last)
    def _():
        o_ref[...] = jnp.zeros_like(o_ref)

    @pl.when(pl.program_id(0) < last)
    def _():
        tm = h_ref.shape[0]
        r1 = o_ref.shape[0]
        sel = jnp.where(wt_ref[...] > 0.0, 1.0, 0.0)
        sel = jnp.concatenate([sel, jnp.zeros((EXPERT_LANES - N_EXPERTS, tm), F32)], axis=0)
        chosen = sel > 0.5
        mask = sel.astype(BF16)
        cnt = _dot(mask, iu_ref[...])
        ones_lanes = jnp.ones((tm, EXPERT_LANES), BF16)
        len_col = _ceil_seg(_dot(mask, ones_lanes))
        off_col = _dot_tn(su_ref[...], len_col.astype(BF16))
        len_row = _ceil_seg(_dot_nt(jnp.ones((8, tm), BF16), mask))
        off_row = _dot(len_row.astype(BF16), su_ref[...])
        dest = jnp.where(chosen, cnt + jnp.concatenate([off_col] * (tm // EXPERT_LANES), axis=1) - 1.0, -1.0)
        dhi, dlo = _split_256(dest)
        rows_e = lax.broadcasted_iota(jnp.int32, (r1, EXPERT_LANES), 0).astype(F32)
        lo = off_row[0:1, :]
        seg = jnp.where(rows_e >= lo, jnp.where(rows_e < lo + len_row[0:1, :], 1.0, 0.0), 0.0)
        segb = seg.astype(BF16)
        want = 256.0 * _dot(segb, dhi) + _dot(segb, dlo)
        rows_t = lax.broadcasted_iota(jnp.int32, (r1, tm), 0).astype(F32)
        onehot = jnp.where(want == rows_t, 1.0, 0.0).astype(BF16)
        xs = _dot(onehot, h_ref[...])
        wtok = wtok_ref[...]
        w_hi = wtok.astype(BF16)
        w_lo = (wtok - w_hi.astype(F32)).astype(BF16)
        w_row = jnp.sum((_dot(onehot, w_hi) + _dot(onehot, w_lo)) * seg, axis=1, keepdims=True)
        o_ref[:, 0:PACK_W] = _pack_bf16_valued_pairs(xs)
        o_ref[:, PACK_W:ROW_W] = pltpu.bitcast(jnp.broadcast_to(w_row, (r1, HEAD_W)), jnp.int32)


def _dispatch(h2, w_t, w_tok):
    n, d = h2.shape
    tiles = n // ROW_TILE
    iu, su = _moe_tables(ROW_TILE)
    last = tiles - 1
    return pl.pallas_call(
        _dispatch_kernel,
        grid=(tiles + 1,),
        in_specs=[pl.BlockSpec((ROW_TILE, d), lambda i: (jnp.minimum(i, last), 0)),
                  pl.BlockSpec((N_EXPERTS, ROW_TILE), lambda i: (0, jnp.minimum(i, last))),
                  pl.BlockSpec((ROW_TILE, EXPERT_LANES), lambda i: (jnp.minimum(i, last), 0)),
                  pl.BlockSpec((ROW_TILE, ROW_TILE), lambda i: (0, 0)),
                  pl.BlockSpec((EXPERT_LANES, EXPERT_LANES), lambda i: (0, 0))],
        out_specs=pl.BlockSpec((TILE_ROWS, ROW_W), lambda i: (i, 0)),
        out_shape=jax.ShapeDtypeStruct(((tiles + 1) * TILE_ROWS, ROW_W), jnp.int32),
        compiler_params=_cparams(("parallel",)),
        name="moe_dispatch",
    )(h2, w_t, w_tok, jnp.asarray(iu, BF16), jnp.asarray(su, BF16))


def _moe_plan(w_t, tiles):
    cb = CHUNKS_PER_BLOCK
    cnt = jnp.sum((w_t > 0.0).reshape(N_EXPERTS, tiles, ROW_TILE), axis=-1, dtype=jnp.int32)
    nch = (cnt + SEG_ALIGN - 1) // SEG_ALIGN
    tile_chunk0 = jnp.cumsum(nch, axis=0) - nch
    per_expert = jnp.sum(nch, axis=1)
    padded = (per_expert + cb - 1) // cb * cb
    expert_end = jnp.cumsum(padded)
    expert_start = expert_end - padded
    seg_start = (expert_start[:, None] + jnp.cumsum(nch, axis=1) - nch).reshape(-1)
    seg_len = nch.reshape(-1)
    seg_src = (jnp.arange(tiles, dtype=jnp.int32)[None, :] * (TILE_ROWS // SEG_ALIGN) + tile_chunk0).reshape(-1)
    n_tok = tiles * ROW_TILE
    max_chunks = n_tok * TOP_K // SEG_ALIGN + tiles * N_EXPERTS + N_EXPERTS * (cb - 1)
    n_blocks = -(-max_chunks // cb)
    per_seg = jnp.stack([seg_start, seg_len, seg_src], axis=1)
    delta = per_seg - jnp.concatenate([jnp.zeros((1, 3), jnp.int32), per_seg[:-1]], axis=0)
    filled = jnp.cumsum(jnp.zeros((n_blocks * cb, 3), jnp.int32).at[seg_start].add(delta, mode='drop'), axis=0)
    c = jnp.arange(n_blocks * cb, dtype=jnp.int32) - filled[:, 0]
    used = c < filled[:, 1]
    spare0 = tiles * (TILE_ROWS // SEG_ALIGN)
    k = jnp.arange(n_blocks * cb, dtype=jnp.int32)
    read_src = jnp.where(used, filled[:, 2] + c, spare0 + 2 * cb).astype(jnp.int32)
    write_dst = jnp.where(used, filled[:, 2] + c, spare0 + k % (2 * cb)).astype(jnp.int32)
    blk = jnp.arange(n_blocks, dtype=jnp.int32) * cb
    blk_e = jnp.minimum(jnp.sum((expert_end[None, :] <= blk[:, None]).astype(jnp.int32), axis=1), N_EXPERTS - 1)
    n_active = (expert_end[-1] // cb).astype(jnp.int32).reshape(1)
    return blk_e, n_active, read_src, write_dst


def _expert_kernel(blk_e_ref, nact_ref, rsrc_ref, wdst_ref, xs_hbm, wg_ref, wu_ref, wd_ref, out_hbm,
                   xbuf, obuf, gsem, ssem):
    del blk_e_ref
    b = pl.program_id(0)
    n_active = nact_ref[0]
    cb = CHUNKS_PER_BLOCK

    def chunk_copies(blk, slot, gather, start):
        for k in range(cb):
            local = pl.ds(k * SEG_ALIGN, SEG_ALIGN)
            if gather:
                ch = rsrc_ref[blk * cb + k]
                rows = pl.ds(pl.multiple_of(ch * SEG_ALIGN, SEG_ALIGN), SEG_ALIGN)
                cp = pltpu.make_async_copy(xs_hbm.at[rows], xbuf.at[slot, local], gsem.at[slot])
            else:
                ch = wdst_ref[blk * cb + k]
                rows = pl.ds(pl.multiple_of(ch * SEG_ALIGN, SEG_ALIGN), SEG_ALIGN)
                cp = pltpu.make_async_copy(obuf.at[slot, local], out_hbm.at[rows], ssem.at[slot])
            if start:
                cp.start()
            else:
                cp.wait()

    @pl.when(b == 0)
    def _():
        chunk_copies(0, 0, True, True)

    @pl.when(b < n_active)
    def _():
        slot = b % 2

        @pl.when(b + 1 < n_active)
        def _():
            chunk_copies(b + 1, 1 - slot, True, True)

        chunk_copies(b, slot, True, False)

        @pl.when(b >= 2)
        def _():
            chunk_copies(b - 2, slot, False, False)

        x = xbuf[slot]
        xb = _unpack_pairs(x[:, 0:PACK_W])
        w = pltpu.bitcast(x[:, PACK_W:ROW_W], F32)
        hid = _silu(_dot(xb, wg_ref[0])) * _dot(xb, wu_ref[0]) * jnp.concatenate([w, w], axis=1)
        obuf[slot, :, 0:PACK_W] = _pack_pairs(_dot(hid.astype(BF16), wd_ref[0]))
        obuf[slot, :, PACK_W:ROW_W] = x[:, PACK_W:ROW_W]
        chunk_copies(b, slot, False, True)

        @pl.when(b == n_active - 1)
        def _():
            chunk_copies(b, slot, False, False)

            @pl.when(b >= 1)
            def _():
                chunk_copies(b - 1, 1 - slot, False, False)


def _experts(xs, plan, wg, wu, wd):
    blk_e, n_active, read_src, write_dst = plan
    n_blocks = blk_e.shape[0]
    d = wg.shape[1]
    wspec = lambda shape: pl.BlockSpec((1,) + shape, lambda b, be, na, rs, wd_: (be[b], 0, 0))
    return pl.pallas_call(
        _expert_kernel,
        grid_spec=pltpu.PrefetchScalarGridSpec(
            num_scalar_prefetch=4,
            grid=(n_blocks,),
            in_specs=[pl.BlockSpec(memory_space=pl.ANY),
                      wspec((d, EXPERT_DIM)), wspec((d, EXPERT_DIM)), wspec((EXPERT_DIM, d))],
            out_specs=pl.BlockSpec(memory_space=pl.ANY),
            scratch_shapes=[pltpu.VMEM((2, EXPERT_BLOCK, ROW_W), xs.dtype),
                            pltpu.VMEM((2, EXPERT_BLOCK, ROW_W), xs.dtype),
                            pltpu.SemaphoreType.DMA((2,)), pltpu.SemaphoreType.DMA((2,))]),
        out_shape=jax.ShapeDtypeStruct(xs.shape, xs.dtype),
        input_output_aliases={4: 0},
        compiler_params=_cparams(("arbitrary",)),
        name="moe_experts",
    )(blk_e, n_active, read_src, write_dst, xs, wg, wu, wd)


def _combine_kernel(ys_ref, wtok_ref, h_ref, x1_ref, mod_ref, ln_ref, il_ref, su_ref,
                    sg_ref, su2_ref, sd_ref, o_ref, *, alpha):
    tm = h_ref.shape[0]
    r1 = ys_ref.shape[0]
    chosen = wtok_ref[...] > 0.0
    mask = jnp.where(chosen, 1.0, 0.0).astype(BF16)
    cnt = _dot(il_ref[...], mask)
    len_row = _ceil_seg(_dot(jnp.ones((8, tm), BF16), mask))
    off_row = _dot(len_row.astype(BF16), su_ref[...])
    dhi, dlo = _split_256(jnp.where(chosen, cnt + off_row[0:1, :] - 1.0, -1.0))
    len_col = _ceil_seg(_dot_tn(mask, jnp.ones((tm, EXPERT_LANES), BF16)))
    off_col = _dot_tn(su_ref[...], len_col.astype(BF16))
    reps = r1 // EXPERT_LANES
    lo = jnp.concatenate([off_col] * reps, axis=1)
    hi = lo + jnp.concatenate([len_col] * reps, axis=1)
    rows_e = lax.broadcasted_iota(jnp.int32, (EXPERT_LANES, r1), 1).astype(F32)
    segb = jnp.where(rows_e >= lo, jnp.where(rows_e < hi, 1.0, 0.0), 0.0).astype(BF16)
    want = 256.0 * _dot(dhi, segb) + _dot(dlo, segb)
    rows_t = lax.broadcasted_iota(jnp.int32, (tm, r1), 1).astype(F32)
    onehot = jnp.where(want == rows_t, 1.0, 0.0).astype(BF16)
    y = _dot(onehot, _unpack_pairs(ys_ref[:, 0:PACK_W]))
    x = h_ref[...]
    y = y + _dot((_silu(_dot(x, sg_ref[...])) * _dot(x, su2_ref[...])).astype(BF16), sd_ref[...])
    gate2 = mod_ref[0, 0, 5:6, :]
    o_ref[...] = _ln(alpha * x1_ref[...] + gate2 * y) * ln_ref[0:1, :] + ln_ref[1:2, :]


def _combine(ys, w_tok, h2, x1, mods, ln2, sg, su, sd, alpha, tiles_per_seq, n_ctx_tiles):
    n, d = h2.shape
    tiles = n // ROW_TILE
    iu, sup = _moe_tables(ROW_TILE)
    full = lambda arr: pl.BlockSpec(arr.shape, lambda i: (0,) * arr.ndim)
    row = pl.BlockSpec((ROW_TILE, d), lambda i: (i, 0))
    il = jnp.asarray(iu.T, BF16)
    sup = jnp.asarray(sup, BF16)
    return pl.pallas_call(
        functools.partial(_combine_kernel, alpha=alpha),
        grid=(tiles,),
        in_specs=[pl.BlockSpec((TILE_ROWS, ROW_W), lambda i: (i, 0)),
                  pl.BlockSpec((ROW_TILE, EXPERT_LANES), lambda i: (i, 0)),
                  row, row,
                  pl.BlockSpec((1, 1, 6, d), lambda i: (i // tiles_per_seq,
                                                        (i % tiles_per_seq >= n_ctx_tiles).astype(jnp.int32), 0, 0)),
                  full(ln2), full(il), full(sup), full(sg), full(su), full(sd)],
        out_specs=row,
        out_shape=jax.ShapeDtypeStruct((n, d), F32),
        compiler_params=_cparams(("parallel",)),
        name="moe_combine",
    )(ys, w_tok, h2, x1, mods, ln2, il, sup, sg, su, sd)


def _rope_tables(n_lat, n_ctx):
    rows = n_lat // GRID_W
    row = jnp.repeat(jnp.arange(rows, dtype=F32), GRID_W)
    col = (jnp.arange(n_lat) % GRID_W).astype(F32)
    n_freq = DA_DH // 4
    inv = ROPE_THETA ** (-jnp.arange(n_freq, dtype=F32) / n_freq)
    ang = jnp.concatenate([row[:, None] * inv, col[:, None] * inv], axis=-1)
    ang = jnp.concatenate([ang, ang], axis=-1)
    cos, sin = jnp.cos(ang), jnp.sin(ang)
    sign = jnp.where(jnp.arange(DA_DH) < DA_DH // 2, -1.0, 1.0).astype(F32)
    cos = jnp.tile(cos, (1, HEAD_W // DA_DH))
    sin = jnp.tile(sin * sign, (1, HEAD_W // DA_DH))
    cos = jnp.concatenate([jnp.ones((n_ctx, HEAD_W), F32), cos], axis=0)
    sin = jnp.concatenate([jnp.zeros((n_ctx, HEAD_W), F32), sin], axis=0)
    return cos, sin


def kernel(x, c, ctx, c_ctx, w_ada, b_ada, w_in, lb_logits, hg_norm_g, lam_params, da_norm_g, w_branch_a, w_branch_b, w_branch_c, w_out, ln1_g, ln1_b, ln2_g, ln2_b, w_router, router_bias, w_exp_gate, w_exp_up, w_exp_down, w_sh_gate, w_sh_up, w_sh_down):
    bsz, n_lat, d = x.shape
    n_ctx = ctx.shape[1]
    depth = w_in.shape[0]
    t = n_ctx + n_lat
    assert n_ctx % ROW_TILE == 0 and n_lat % ROW_TILE == 0
    assert n_ctx % HG_CHUNK == 0 and n_lat % HG_CHUNK == 0 and n_lat % GRID_W == 0
    n_ctx_tiles = n_ctx // ROW_TILE
    alpha = (2 * depth) ** 0.25
    per = N_EXPERTS // N_GROUPS

    cos, sin = _rope_tables(n_lat, n_ctx)
    sm = jax.nn.softmax(lb_logits.astype(F32), axis=0)
    lower = jnp.cumsum(sm, axis=0) - sm[0:1]
    log_lb = jnp.log(jnp.maximum(lower, LB_FLOOR))
    log1m_lb = jnp.log1p(-lower)

    n_rows = -(-(bsz + 1) // 8) * 8
    cvec = jnp.concatenate([c, c_ctx[None, :], jnp.zeros((n_rows - bsz - 1, d), F32)], axis=0)

    n_branch_cols = w_in.shape[2] - 3 * d

    def member_major(a):
        return a.reshape((N_GROUPS, per) + a.shape[1:]).swapaxes(0, 1).reshape(a.shape)

    xs = jnp.concatenate([ctx, x], axis=1)
    for l in range(depth):
        mod = _ada(cvec, w_ada[l], b_ada[l])
        mod_lat = mod[:bsz].reshape(bsz, 1, 6, d)
        mod_ctx = jnp.broadcast_to(mod[bsz].reshape(1, 1, 6, d), (bsz, 1, 6, d))
        mods = jnp.concatenate([mod_ctx, mod_lat], axis=1)

        w_in_l = jnp.concatenate([w_in[l][:, n_branch_cols:], w_in[l][:, :n_branch_cols]], axis=1)
        p = _inproj(xs, mods, w_in_l.astype(BF16), n_ctx)
        lb_f = jnp.stack([log_lb[l, 0], log1m_lb[l, 0]])
        lb_b = jnp.stack([log_lb[l, 1], log1m_lb[l, 1]])
        o_f, o_b = _hgrn(p, lb_f, lb_b, n_ctx)

        lam_init = 0.8 - 0.6 * math.exp(-0.3 * l)
        lp = lam_params[l].astype(F32)
        lam = jnp.exp(jnp.sum(lp[0] * lp[1])) - jnp.exp(jnp.sum(lp[2] * lp[3])) + lam_init
        qr, kr, vr = _attn_prep(p, cos, sin)
        at = _attention(qr, kr, vr, jnp.full((8, HEAD_W), lam, F32), n_ctx)
        fr = _fourier(p, n_ctx)

        skip = n_ctx_tiles if l == depth - 1 else 0
        ln1 = jnp.stack([ln1_g[l], ln1_b[l]])
        hg_g = hg_norm_g[l].reshape(1, HEAD_W)
        da_g = da_norm_g[l].reshape(1, HEAD_W)
        x1, h2, logits_t = _merge(
            xs, mods, p, o_f, o_b, at, fr, hg_g, da_g,
            w_branch_a[l].astype(BF16), w_branch_b[l].astype(BF16), w_branch_c[l].astype(BF16),
            w_out[l].astype(BF16), ln1, member_major(w_router[l].T), alpha, 1.0 - lam_init,
            n_ctx_tiles, skip)

        bias_col = jnp.broadcast_to(member_major(router_bias[l])[:, None], (N_EXPERTS, ROUTER_TILE))
        wts_t = _router(logits_t, bias_col)
        t_out = x1.shape[1]
        n_tok = bsz * t_out
        w_tok = jnp.concatenate([wts_t.T, jnp.zeros((n_tok, EXPERT_LANES - N_EXPERTS), F32)], axis=1)
        h2f = h2.reshape(n_tok, d)
        rows = _dispatch(h2f, wts_t, w_tok)
        plan = _moe_plan(wts_t, n_tok // ROW_TILE)
        rows = _experts(rows, plan, member_major(w_exp_gate[l]).astype(BF16),
                        member_major(w_exp_up[l]).astype(BF16), member_major(w_exp_down[l]).astype(BF16))
        ln2 = jnp.stack([ln2_g[l], ln2_b[l]])
        xs = _combine(rows, w_tok, h2f, x1.reshape(n_tok, d), mods, ln2,
                      w_sh_gate[l].astype(BF16), w_sh_up[l].astype(BF16), w_sh_down[l].astype(BF16),
                      alpha, t_out // ROW_TILE, n_ctx_tiles - skip).reshape(bsz, t_out, d)
    return xs if xs.shape[1] == n_lat else xs[:, n_ctx:]
```
